```python
import math
import jax
import jax.numpy as jnp
from jax import lax
import numpy as np

D_MODEL = 2048
BATCH = 2
SEQ = 16384
DEPTH = 4

N_MIXERS = 3
N_MEM = 256
NORM_EPS = 1e-6

SSD_EXPAND = 2
SSD_INNER = SSD_EXPAND * D_MODEL
SSD_HEADDIM = 64
SSD_HEADS = SSD_INNER // SSD_HEADDIM
SSD_STATE = 128
SSD_GROUPS = 8
SSD_CONV = 4
SSD_CHUNK = 128
SSD_CONV_DIM = SSD_INNER + 2 * SSD_GROUPS * SSD_STATE
SSD_IN_DIM = SSD_INNER + SSD_CONV_DIM + 2 * SSD_HEADS
DT_MIN = 1e-3
DT_MAX = 1e-1

MLSTM_HEADS = 8
MLSTM_QK_DIM = D_MODEL // 2
MLSTM_V_DIM = D_MODEL
MLSTM_QK_HEAD = MLSTM_QK_DIM // MLSTM_HEADS
MLSTM_V_HEAD = MLSTM_V_DIM // MLSTM_HEADS
MLSTM_CHUNK = 128
MLSTM_IN_DIM = 2 * MLSTM_QK_DIM + 2 * MLSTM_V_DIM + 4 * MLSTM_HEADS

RG_WIDTH = D_MODEL
RG_BLOCKS = 8
RG_BLOCK_DIM = RG_WIDTH // RG_BLOCKS
RG_CONV = 4
RG_C = 8.0
RG_A_MIN = 0.9
RG_A_MAX = 0.999

XA_HEADS = 4
XA_HEAD_DIM = D_MODEL // XA_HEADS

FFN_DIM = 5632
FFN_CONV = 3

kernel_name = "bidir_hybrid_ssd_mlstm_rglru_trunk"


def rmsnorm(x, g):
    xf = x.astype(jnp.float32)
    y = xf * lax.rsqrt(jnp.mean(xf * xf, axis=-1, keepdims=True) + NORM_EPS)
    return y.astype(x.dtype) * g


def dwconv(x, w, b):
    width = w.shape[0]
    left = width // 2
    out = lax.conv_general_dilated(
        x, w[:, None, :].astype(x.dtype), window_strides=(1,),
        padding=[(left, width - 1 - left)],
        dimension_numbers=("NWC", "WIO", "NWC"),
        feature_group_count=x.shape[-1])
    return out + b


def flip(t):
    return jnp.flip(t, axis=1)


def ssd_scan(x, dt, A, B, C):
    b, s, h, p = x.shape
    g, n = B.shape[2], B.shape[3]
    k = h // g
    L = SSD_CHUNK
    nc = s // L
    xd = (x * dt[..., None]).reshape(b, nc, L, g, k, p)
    Bc = B.reshape(b, nc, L, g, n)
    Cc = C.reshape(b, nc, L, g, n)
    la = jnp.moveaxis((dt * A).reshape(b, nc, L, g, k), 2, -1)
    cum = jnp.cumsum(la, axis=-1)
    tril = jnp.tril(jnp.ones((L, L), dtype=bool))
    seg = jnp.exp(jnp.where(tril, cum[..., :, None] - cum[..., None, :], -jnp.inf))
    cb = jnp.einsum("bclgn,bcmgn->bcglm", Cc, Bc)
    y_diag = jnp.einsum("bcglm,bcgklm,bcmgkp->bclgkp", cb, seg, xd)
    decay_to_end = jnp.exp(cum[..., -1:] - cum)
    states = jnp.einsum("bclgn,bcgkl,bclgkp->bcgkpn", Bc, decay_to_end, xd)
    chunk_decay = jnp.exp(cum[..., -1])

    def step(carry, inp):
        st, dec = inp
        return carry * dec[..., None, None] + st, carry

    init = jnp.zeros_like(states[:, 0])
    _, prev = lax.scan(step, init, (jnp.moveaxis(states, 1, 0), jnp.moveaxis(chunk_decay, 1, 0)))
    prev = jnp.moveaxis(prev, 0, 1)
    y_off = jnp.einsum("bclgn,bcgkpn,bcgkl->bclgkp", Cc, prev, jnp.exp(cum))
    return (y_diag + y_off).reshape(b, s, h, p)


def ssd_mixer(u, w_in, conv_w, conv_b, dt_bias, a_log, d_skip, norm_g, w_out):
    b, s, _ = u.shape
    proj = u @ w_in
    z, xbc, dt_raw = jnp.split(proj, [SSD_INNER, SSD_INNER + SSD_CONV_DIM], axis=-1)
    xbc = jax.nn.silu(dwconv(xbc, conv_w, conv_b))
    xs, Bm, Cm = jnp.split(xbc, [SSD_INNER, SSD_INNER + SSD_GROUPS * SSD_STATE], axis=-1)
    xs = xs.reshape(b, s, SSD_HEADS, SSD_HEADDIM)
    Bm = Bm.reshape(b, s, SSD_GROUPS, SSD_STATE)
    Cm = Cm.reshape(b, s, SSD_GROUPS, SSD_STATE)
    dt = jax.nn.softplus(dt_raw.astype(jnp.float32).reshape(b, s, 2, SSD_HEADS) + dt_bias)
    A = -jnp.exp(a_log.astype(jnp.float32))
    y_f = ssd_scan(xs, dt[:, :, 0], A[0], Bm, Cm)
    y_b = flip(ssd_scan(flip(xs), flip(dt[:, :, 1]), A[1], flip(Bm), flip(Cm)))
    y = y_f + y_b + xs * d_skip[:, None]
    y = rmsnorm(y.reshape(b, s, SSD_INNER) * jax.nn.silu(z), norm_g)
    return y.astype(u.dtype) @ w_out


def mlstm_chunked(q, k, v, ig, fg):
    b, s, h, dk = q.shape
    dv = v.shape[-1]
    L = MLSTM_CHUNK
    nc = s // L
    qc = q.reshape(b, nc, L, h, dk)
    kc = k.reshape(b, nc, L, h, dk)
    vc = v.reshape(b, nc, L, h, dv)
    li = jnp.moveaxis(ig.reshape(b, nc, L, h), 2, -1)
    lf = jnp.moveaxis(jax.nn.log_sigmoid(fg).reshape(b, nc, L, h), 2, -1)
    F = jnp.cumsum(lf, axis=-1)
    G = F[..., -1]
    a = G[..., None] - F + li
    a_max = jnp.max(a, axis=-1)
    w = jnp.exp(a - a_max[..., None])
    kv_loc = jnp.einsum("bchl,bclhk,bclhv->bchkv", w, kc, vc)
    n_loc = jnp.einsum("bchl,bclhk->bchk", w, kc)

    def step(carry, inp):
        Cs, ns, m = carry
        g_c, amax_c, kv_c, n_c = inp
        m_new = jnp.maximum(g_c + m, amax_c)
        s_old = jnp.exp(g_c + m - m_new)
        s_new = jnp.exp(amax_c - m_new)
        C_new = s_old[..., None, None] * Cs + s_new[..., None, None] * kv_c
        n_new = s_old[..., None] * ns + s_new[..., None] * n_c
        return (C_new, n_new, m_new), (Cs, ns, m)

    init = (jnp.zeros_like(kv_loc[:, 0]), jnp.zeros_like(n_loc[:, 0]), jnp.zeros_like(G[:, 0]))
    xs = (jnp.moveaxis(G, 1, 0), jnp.moveaxis(a_max, 1, 0),
          jnp.moveaxis(kv_loc, 1, 0), jnp.moveaxis(n_loc, 1, 0))
    _, (C_prev, n_prev, m_prev) = lax.scan(step, init, xs)
    C_prev = jnp.moveaxis(C_prev, 0, 1)
    n_prev = jnp.moveaxis(n_prev, 0, 1)
    m_prev = jnp.moveaxis(m_prev, 0, 1)

    tril = jnp.tril(jnp.ones((L, L), dtype=bool))
    Dm = jnp.where(tril, F[..., :, None] - F[..., None, :] + li[..., None, :], -jnp.inf)
    m_inter = F + m_prev[..., None]
    m_t = jnp.maximum(m_inter, jnp.max(Dm, axis=-1))
    S = jnp.exp(Dm - m_t[..., None]) * jnp.einsum("bclhk,bcmhk->bchlm", qc, kc)
    inter = jnp.exp(m_inter - m_t)
    num = (jnp.einsum("bchlm,bcmhv->bchlv", S, vc)
           + inter[..., None] * jnp.einsum("bclhk,bchkv->bchlv", qc, C_prev))
    den = jnp.sum(S, axis=-1) + inter * jnp.einsum("bclhk,bchk->bchl", qc, n_prev)
    hout = num / jnp.maximum(jnp.abs(den), jnp.exp(-m_t))[..., None]
    return jnp.moveaxis(hout, 2, 3).reshape(b, s, h, dv)


def mlstm_mixer(u, w_in, gate_bias, head_norm, w_out):
    b, s, _ = u.shape
    proj = u @ w_in
    q, k, v, o, gates = jnp.split(
        proj, [MLSTM_QK_DIM, 2 * MLSTM_QK_DIM, 2 * MLSTM_QK_DIM + MLSTM_V_DIM,
               2 * MLSTM_QK_DIM + 2 * MLSTM_V_DIM], axis=-1)
    q = q.reshape(b, s, MLSTM_HEADS, MLSTM_QK_HEAD)
    k = k.reshape(b, s, MLSTM_HEADS, MLSTM_QK_HEAD) * MLSTM_QK_HEAD ** -0.5
    v = v.reshape(b, s, MLSTM_HEADS, MLSTM_V_HEAD)
    gates = gates.astype(jnp.float32).reshape(b, s, 4, MLSTM_HEADS) + gate_bias
    h_f = mlstm_chunked(q, k, v, gates[:, :, 0], gates[:, :, 1])
    h_b = flip(mlstm_chunked(flip(q), flip(k), flip(v), flip(gates[:, :, 2]), flip(gates[:, :, 3])))
    hh = rmsnorm(h_f + h_b, head_norm.reshape(MLSTM_HEADS, MLSTM_V_HEAD))
    hh = hh.reshape(b, s, MLSTM_V_DIM) * jax.nn.sigmoid(o)
    return hh.astype(u.dtype) @ w_out


def linear_scan(a, bx):
    def combine(l, r):
        return (l[0] * r[0], r[0] * l[1] + r[1])
    _, hs = lax.associative_scan(combine, (a, bx), axis=1)
    return hs


def rglru_mixer(u, w_in, conv_w, conv_b, gate_w, gate_b, lam, w_out):
    b, s, _ = u.shape
    gate_br, xr = jnp.split(u @ w_in, [RG_WIDTH], axis=-1)
    xr = dwconv(xr, conv_w, conv_b)
    xb = xr.reshape(b, s, RG_BLOCKS, RG_BLOCK_DIM)
    hsum = jnp.zeros((b, s, RG_WIDTH), jnp.float32)
    for d in range(2):
        g = (jnp.einsum("bsnk,nkj->bsnj", xb, gate_w[d]) + gate_b[d]).astype(jnp.float32)
        r = jax.nn.sigmoid(g[..., :RG_BLOCK_DIM]).reshape(b, s, RG_WIDTH)
        i = jax.nn.sigmoid(g[..., RG_BLOCK_DIM:]).reshape(b, s, RG_WIDTH)
        log_a = -RG_C * r * jax.nn.softplus(-lam[d].astype(jnp.float32))
        a = jnp.exp(log_a)
        bx = jnp.sqrt(-jnp.expm1(2.0 * log_a)) * (i * xr)
        if d == 0:
            hsum = hsum + linear_scan(a, bx)
        else:
            hsum = hsum + flip(linear_scan(flip(a), flip(bx)))
    y = hsum * jax.nn.gelu(gate_br)
    return y.astype(u.dtype) @ w_out


def mem_cross_attention(u, mem_n, wq, wkv, wo):
    b, s, _ = u.shape
    m = mem_n.shape[1]
    q = (u @ wq).reshape(b, s, XA_HEADS, XA_HEAD_DIM)
    k, v = jnp.split(mem_n @ wkv, 2, axis=-1)
    k = k.reshape(b, m, XA_HEADS, XA_HEAD_DIM)
    v = v.reshape(b, m, XA_HEADS, XA_HEAD_DIM)
    scores = jnp.einsum("bshd,bmhd->bhsm", q, k).astype(jnp.float32) * XA_HEAD_DIM ** -0.5
    p = jax.nn.softmax(scores, axis=-1).astype(v.dtype)
    o = jnp.einsum("bhsm,bmhd->bshd", p, v).reshape(b, s, D_MODEL)
    return o @ wo


def conv_glu_ffn(u, w_up, conv_w, conv_b, w_down):
    gate, val = jnp.split(u @ w_up, 2, axis=-1)
    return (jax.nn.silu(dwconv(gate, conv_w, conv_b)) * val) @ w_down


def setup_inputs(seed: int = 0) -> dict:
    key = jax.random.key(seed)
    ks = iter(jax.random.split(key, 64))

    def normal(shape, scale):
        return jax.random.normal(next(ks), shape, jnp.float32) * scale

    def gain(shape):
        return 1.0 + normal(shape, 0.02)

    n_ssd = len(range(0, DEPTH, N_MIXERS))
    n_mlstm = len(range(1, DEPTH, N_MIXERS))
    n_rglru = len(range(2, DEPTH, N_MIXERS))

    dt0 = jnp.exp(jax.random.uniform(next(ks), (n_ssd, 2, SSD_HEADS), jnp.float32,
                                     math.log(DT_MIN), math.log(DT_MAX)))
    ssd_dt_bias = dt0 + jnp.log(-jnp.expm1(-dt0))
    ssd_a_log = jnp.log(jax.random.uniform(next(ks), (n_ssd, 2, SSD_HEADS), jnp.float32, 1.0, 16.0))
    a_c = jax.random.uniform(next(ks), (n_rglru, 2, RG_WIDTH), jnp.float32, RG_A_MIN, RG_A_MAX)
    base = a_c ** (1.0 / RG_C)
    rglru_lambda = jnp.log(base) - jnp.log1p(-base)
    mlstm_gate_bias = (jnp.array([0.0, 3.0, 0.0, 3.0], jnp.float32)[None, :, None]
                       + normal((n_mlstm, 4, MLSTM_HEADS), 0.1))

    D = D_MODEL
    return {
        "x": normal((BATCH, SEQ, D), 1.0),
        "mem": normal((BATCH, N_MEM, D), 1.0),
        "mem_norm": gain((D,)),
        "mix_norm": gain((DEPTH, D)),
        "xattn_norm": gain((DEPTH, D)),
        "xattn_wq": normal((DEPTH, D, D), D ** -0.5),
        "xattn_wkv": normal((DEPTH, D, 2 * D), D ** -0.5),
        "xattn_wo": normal((DEPTH, D, D), D ** -0.5),
        "ffn_norm": gain((DEPTH, D)),
        "ffn_w_up": normal((DEPTH, D, 2 * FFN_DIM), D ** -0.5),
        "ffn_conv_w": normal((DEPTH, FFN_CONV, FFN_DIM), FFN_CONV ** -0.5),
        "ffn_conv_b": normal((DEPTH, FFN_DIM), 0.02),
        "ffn_w_down": normal((DEPTH, FFN_DIM, D), FFN_DIM ** -0.5),
        "ssd_w_in": normal((n_ssd, D, SSD_IN_DIM), D ** -0.5),
        "ssd_conv_w": normal((n_ssd, SSD_CONV, SSD_CONV_DIM), SSD_CONV ** -0.5),
        "ssd_conv_b": normal((n_ssd, SSD_CONV_DIM), 0.02),
        "ssd_dt_bias": ssd_dt_bias,
        "ssd_a_log": ssd_a_log,
        "ssd_d_skip": 1.0 + normal((n_ssd, SSD_HEADS), 0.1),
        "ssd_norm": gain((n_ssd, SSD_INNER)),
        "ssd_w_out": normal((n_ssd, SSD_INNER, D), SSD_INNER ** -0.5),
        "mlstm_w_in": normal((n_mlstm, D, MLSTM_IN_DIM), D ** -0.5),
        "mlstm_gate_bias": mlstm_gate_bias,
        "mlstm_head_norm": gain((n_mlstm, MLSTM_V_DIM)),
        "mlstm_w_out": normal((n_mlstm, MLSTM_V_DIM, D), MLSTM_V_DIM ** -0.5),
        "rglru_w_in": normal((n_rglru, D, 2 * RG_WIDTH), D ** -0.5),
        "rglru_conv_w": normal((n_rglru, RG_CONV, RG_WIDTH), RG_CONV ** -0.5),
        "rglru_conv_b": normal((n_rglru, RG_WIDTH), 0.02),
        "rglru_gate_w": normal((n_rglru, 2, RG_BLOCKS, RG_BLOCK_DIM, 2 * RG_BLOCK_DIM), RG_BLOCK_DIM ** -0.5),
        "rglru_gate_b": normal((n_rglru, 2, RG_BLOCKS, 2 * RG_BLOCK_DIM), 0.02),
        "rglru_lambda": rglru_lambda,
        "rglru_w_out": normal((n_rglru, RG_WIDTH, D), RG_WIDTH ** -0.5),
        "final_norm": gain((D,)),
    }


def reference(x, mem, mem_norm, mix_norm, xattn_norm, xattn_wq, xattn_wkv, xattn_wo,
              ffn_norm, ffn_w_up, ffn_conv_w, ffn_conv_b, ffn_w_down,
              ssd_w_in, ssd_conv_w, ssd_conv_b, ssd_dt_bias, ssd_a_log, ssd_d_skip, ssd_norm, ssd_w_out,
              mlstm_w_in, mlstm_gate_bias, mlstm_head_norm, mlstm_w_out,
              rglru_w_in, rglru_conv_w, rglru_conv_b, rglru_gate_w, rglru_gate_b, rglru_lambda, rglru_w_out,
              final_norm):
    mem_n = rmsnorm(mem, mem_norm)
    h = x
    for i in range(DEPTH):
        kind = i % N_MIXERS
        j = i // N_MIXERS
        u = rmsnorm(h, mix_norm[i])
        if kind == 0:
            mix = ssd_mixer(u, ssd_w_in[j], ssd_conv_w[j], ssd_conv_b[j], ssd_dt_bias[j],
                            ssd_a_log[j], ssd_d_skip[j], ssd_norm[j], ssd_w_out[j])
        elif kind == 1:
            mix = mlstm_mixer(u, mlstm_w_in[j], mlstm_gate_bias[j], mlstm_head_norm[j], mlstm_w_out[j])
        else:
            mix = rglru_mixer(u, rglru_w_in[j], rglru_conv_w[j], rglru_conv_b[j], rglru_gate_w[j],
                              rglru_gate_b[j], rglru_lambda[j], rglru_w_out[j])
        h = h + mix
        h = h + mem_cross_attention(rmsnorm(h, xattn_norm[i]), mem_n,
                                    xattn_wq[i], xattn_wkv[i], xattn_wo[i])
        h = h + conv_glu_ffn(rmsnorm(h, ffn_norm[i]), ffn_w_up[i], ffn_conv_w[i],
                             ffn_conv_b[i], ffn_w_down[i])
    return rmsnorm(h, final_norm)
```

```python
import functools

import jax
import jax.numpy as jnp
from jax import lax
from jax.experimental import pallas as pl
from jax.experimental.pallas import tpu as pltpu

F32 = jnp.float32
BF16 = jnp.bfloat16

NORM_EPS = 1e-6
CHUNK = 128
HALO = 16
V7X_VMEM_LIMIT = 56 * 1024 * 1024
NEG_BIG = -1e30

SSD_HEADS = 64
SSD_HEADDIM = 64
SSD_STATE = 128
SSD_GROUPS = 8
SSD_INNER = SSD_HEADS * SSD_HEADDIM
MLSTM_HEADS = 8
MLSTM_QK_HEAD = 128
MLSTM_V_HEAD = 256
RG_BLOCKS = 8
RG_BLOCK_DIM = 256
RG_C = 8.0
XA_HEADS = 4
XA_HEAD_DIM = 512


def _params(*sem):
    return pltpu.CompilerParams(dimension_semantics=sem, vmem_limit_bytes=V7X_VMEM_LIMIT)


def _sigmoid(x):
    return 1.0 / (1.0 + jnp.exp(-x))


def _softplus(x):
    return jnp.maximum(x, 0.0) + jnp.log(1.0 + jnp.exp(-jnp.abs(x)))


def _silu(x):
    return x * _sigmoid(x)


def _dot(a, b):
    return jnp.dot(a, b, preferred_element_type=F32)


def _dot_f32(a, b):
    return jnp.dot(a, b, preferred_element_type=F32, precision=lax.Precision.HIGHEST)


def _dot_nt(a, b):
    return lax.dot_general(a, b, (((1,), (1,)), ((), ())), preferred_element_type=F32)


def _dot_tn(a, b):
    return lax.dot_general(a, b, (((0,), (0,)), ((), ())), preferred_element_type=F32)


def _lane_bcast(col, width):
    return jnp.broadcast_to(col, (col.shape[0], width))


def _mm_body(*refs, has_gain, has_resid, stage_lhs):
    it = iter(refs)
    a_ref = next(it)
    g_ref = next(it) if has_gain else None
    w_ref = next(it)
    r_ref = next(it) if has_resid else None
    o_ref = next(it)
    lhs_ref = next(it) if stage_lhs else None

    if stage_lhs:
        @pl.when(pl.program_id(1) == 0)
        def _():
            a = a_ref[...].astype(F32)
            if has_gain:
                ms = jnp.mean(a * a, axis=-1, keepdims=True)
                a = a * lax.rsqrt(ms + NORM_EPS) * g_ref[...]
            lhs_ref[...] = a.astype(BF16)
        lhs = lhs_ref[...]
    else:
        lhs = a_ref[...]
    acc = _dot(lhs, w_ref[...])
    if has_resid:
        acc = acc + r_ref[...]
    o_ref[...] = acc.astype(o_ref.dtype)


def fused_matmul(a, w, *, gain=None, resid=None, out_dtype, tm, tn):
    m, k = a.shape
    n = w.shape[1]
    tm = min(tm, m)
    tn = min(tn, n)
    assert m % tm == 0 and n % tn == 0, (m, n, tm, tn)
    has_gain = gain is not None
    has_resid = resid is not None
    stage_lhs = has_gain or a.dtype != BF16
    in_specs = [pl.BlockSpec((tm, k), lambda i, j: (i, 0))]
    args = [a]
    if has_gain:
        in_specs.append(pl.BlockSpec((1, k), lambda i, j: (0, 0)))
        args.append(gain.reshape(1, k).astype(F32))
    in_specs.append(pl.BlockSpec((k, tn), lambda i, j: (0, j)))
    args.append(w)
    if has_resid:
        in_specs.append(pl.BlockSpec((tm, tn), lambda i, j: (i, j)))
        args.append(resid)
    scratch = [pltpu.VMEM((tm, k), BF16)] if stage_lhs else []
    body = functools.partial(_mm_body, has_gain=has_gain, has_resid=has_resid, stage_lhs=stage_lhs)
    return pl.pallas_call(
        body,
        grid=(m // tm, n // tn),
        in_specs=in_specs,
        out_specs=pl.BlockSpec((tm, tn), lambda i, j: (i, j)),
        out_shape=jax.ShapeDtypeStruct((m, n), out_dtype),
        scratch_shapes=scratch,
        compiler_params=_params("parallel", "arbitrary"),
    )(*args)


def _dwconv_tile(x, prev8, next8, w, left):
    rows = x.shape[0]
    width = w.shape[0]

    def taps(v):
        nrow = v.shape[0]
        acc = None
        for t in range(width):
            shift = (left - t) % nrow
            src = v if shift == 0 else pltpu.roll(v, shift, 0)
            term = src * w[t:t + 1, :]
            acc = term if acc is None else acc + term
        return acc

    body = taps(x)
    top = taps(jnp.concatenate([prev8, x[0:24]], axis=0))[8:16]
    bot = taps(jnp.concatenate([x[rows - 24:rows], next8], axis=0))[16:24]
    return jnp.concatenate([top, body[8:rows - 8], bot], axis=0)


def _halo_rows(prev_ref, next_ref, tile_idx, tile_rows, seq):
    start = tile_idx * tile_rows
    keep_prev = (start % seq != 0).astype(F32)
    keep_next = ((start + tile_rows) % seq != 0).astype(F32)
    prev8 = prev_ref[HALO - 8:HALO, :].astype(F32) * keep_prev
    next8 = next_ref[0:8, :].astype(F32) * keep_next
    return prev8, next8


def _halo_specs(tile_rows, total_rows, width, col_of):
    per = tile_rows // HALO
    last = total_rows // HALO - 1
    prev = pl.BlockSpec((HALO, width), lambda *g: (jnp.maximum(g[0] * per - 1, 0), col_of(*g)))
    nxt = pl.BlockSpec((HALO, width), lambda *g: (jnp.minimum((g[0] + 1) * per, last), col_of(*g)))
    return prev, nxt


def _ssd_pre_body(x_ref, prev_ref, next_ref, w_ref, b_ref, o_ref, *, tile_rows, seq):
    prev8, next8 = _halo_rows(prev_ref, next_ref, pl.program_id(0), tile_rows, seq)
    y = _dwconv_tile(x_ref[...].astype(F32), prev8, next8, w_ref[...], 2) + b_ref[...]
    o_ref[...] = _silu(y).astype(o_ref.dtype)


def ssd_pre(proj, conv_w, conv_b, seq, *, tile_rows=256, tc=1024):
    total = proj.shape[0]
    cdim = conv_w.shape[1]
    col0 = SSD_INNER // tc
    prev_spec, next_spec = _halo_specs(tile_rows, total, tc, lambda i, c: col0 + c)
    body = functools.partial(_ssd_pre_body, tile_rows=tile_rows, seq=seq)
    return pl.pallas_call(
        body,
        grid=(total // tile_rows, cdim // tc),
        in_specs=[
            pl.BlockSpec((tile_rows, tc), lambda i, c: (i, col0 + c)),
            prev_spec, next_spec,
            pl.BlockSpec((4, tc), lambda i, c: (0, c)),
            pl.BlockSpec((1, tc), lambda i, c: (0, c)),
        ],
        out_specs=pl.BlockSpec((tile_rows, tc), lambda i, c: (i, c)),
        out_shape=jax.ShapeDtypeStruct((total, cdim), BF16),
        compiler_params=_params("parallel", "parallel"),
    )(proj, proj, proj, conv_w, conv_b.reshape(1, cdim))


def _ssd_scan_body(*refs, reverse, final):
    if final:
        (x_ref, b_ref, c_ref, dt_ref, dtb_ref, alog_ref, yb_ref, z_ref, dskip_ref,
         o_ref, state_ref) = refs
    else:
        x_ref, b_ref, c_ref, dt_ref, dtb_ref, alog_ref, o_ref, state_ref = refs
    L = CHUNK
    P2 = 2 * SSD_HEADDIM
    GW = SSD_HEADS // SSD_GROUPS * SSD_HEADDIM
    off = SSD_HEADS if reverse else 0
    edge = 0 if reverse else L - 1

    @pl.when(pl.program_id(1) == 0)
    def _():
        state_ref[...] = jnp.zeros_like(state_ref)

    row = lax.broadcasted_iota(jnp.int32, (L, L), 0)
    col = lax.broadcasted_iota(jnp.int32, (L, L), 1)
    causal = (row <= col) if reverse else (row >= col)
    lane_lo = lax.broadcasted_iota(jnp.int32, (L, P2), 1) < SSD_HEADDIM
    lane_lo_row = lane_lo[0:1, :]

    dt = _softplus(dt_ref[...] + dtb_ref[...])
    la = dt * (-jnp.exp(alog_ref[...]))
    cum = _dot_f32(causal.astype(F32), la)
    cum_t = cum.T
    dt_t = dt.T
    total_row = cum[edge:edge + 1, :]
    ecum = jnp.exp(cum)
    wcol = jnp.exp(total_row - cum) * dt

    for g in range(SSD_GROUPS):
        bg = b_ref[:, g * SSD_STATE:(g + 1) * SSD_STATE]
        cg = c_ref[:, g * SSD_STATE:(g + 1) * SSD_STATE]
        cb = _dot_nt(cg, bg)
        s_old = state_ref[:, g * GW:(g + 1) * GW]
        y_prev = _dot(cg, s_old.astype(BF16))
        xw_parts = []
        decay_parts = []
        for j in range(GW // P2):
            lo = g * GW + j * P2
            h1 = off + (lo // SSD_HEADDIM)
            h2 = h1 + 1
            x2 = x_ref[:, lo:lo + P2]
            m_parts = []
            for h in (h1, h2):
                e_col = _lane_bcast(cum[:, h:h + 1], L)
                diff = jnp.where(causal, e_col - cum_t[h:h + 1, :], NEG_BIG)
                m_parts.append((cb * jnp.exp(diff) * dt_t[h:h + 1, :]).astype(BF16))
            lhs = jnp.concatenate(m_parts, axis=1)
            zero = jnp.zeros_like(x2)
            rhs = jnp.concatenate([jnp.where(lane_lo, x2, zero), jnp.where(lane_lo, zero, x2)], axis=0)
            y = _dot(lhs, rhs)
            e_pair = jnp.where(lane_lo, _lane_bcast(ecum[:, h1:h1 + 1], P2),
                               _lane_bcast(ecum[:, h2:h2 + 1], P2))
            y = y + e_pair * y_prev[:, j * P2:(j + 1) * P2]
            w_pair = jnp.where(lane_lo, _lane_bcast(wcol[:, h1:h1 + 1], P2),
                               _lane_bcast(wcol[:, h2:h2 + 1], P2))
            x2f = x2.astype(F32)
            xw_parts.append((x2f * w_pair).astype(BF16))
            decay_parts.append(jnp.exp(jnp.where(lane_lo_row, _lane_bcast(total_row[:, h1:h1 + 1], P2),
                                                 _lane_bcast(total_row[:, h2:h2 + 1], P2))))
            if final:
                y = y + yb_ref[:, lo:lo + P2] + x2f * dskip_ref[:, lo:lo + P2]
                y = y * _silu(z_ref[:, lo:lo + P2].astype(F32))
            o_ref[:, lo:lo + P2] = y.astype(o_ref.dtype)
        s_loc = _dot_tn(bg, jnp.concatenate(xw_parts, axis=1))
        state_ref[:, g * GW:(g + 1) * GW] = s_old * jnp.concatenate(decay_parts, axis=1) + s_loc


def ssd_scan(xbc, dt_raw, dt_bias, a_log, batch, seq, *, reverse, y_back=None, proj=None, d_skip=None):
    final = y_back is not None
    total = xbc.shape[0]
    nc = seq // CHUNK
    bw = SSD_GROUPS * SSD_STATE

    def rows(b, c):
        return b * nc + (nc - 1 - c if reverse else c)

    in_specs = [
        pl.BlockSpec((CHUNK, SSD_INNER), lambda b, c: (rows(b, c), 0)),
        pl.BlockSpec((CHUNK, bw), lambda b, c: (rows(b, c), SSD_INNER // bw)),
        pl.BlockSpec((CHUNK, bw), lambda b, c: (rows(b, c), SSD_INNER // bw + 1)),
        pl.BlockSpec((CHUNK, 2 * SSD_HEADS), lambda b, c: (rows(b, c), 0)),
        pl.BlockSpec((1, 2 * SSD_HEADS), lambda b, c: (0, 0)),
        pl.BlockSpec((1, 2 * SSD_HEADS), lambda b, c: (0, 0)),
    ]
    args = [xbc, xbc, xbc, dt_raw, dt_bias.reshape(1, -1), a_log.reshape(1, -1)]
    if final:
        in_specs += [
            pl.BlockSpec((CHUNK, SSD_INNER), lambda b, c: (rows(b, c), 0)),
            pl.BlockSpec((CHUNK, SSD_INNER), lambda b, c: (rows(b, c), 0)),
            pl.BlockSpec((1, SSD_INNER), lambda b, c: (0, 0)),
        ]
        args += [y_back, proj, jnp.repeat(d_skip, SSD_HEADDIM).reshape(1, SSD_INNER)]
    body = functools.partial(_ssd_scan_body, reverse=reverse, final=final)
    return pl.pallas_call(
        body,
        grid=(batch, nc),
        in_specs=in_specs,
        out_specs=pl.BlockSpec((CHUNK, SSD_INNER), lambda b, c: (rows(b, c), 0)),
        out_shape=jax.ShapeDtypeStruct((total, SSD_INNER), BF16 if final else F32),
        scratch_shapes=[pltpu.VMEM((SSD_STATE, SSD_INNER), F32)],
        compiler_params=_params("parallel", "arbitrary"),
    )(*args)


def ssd_mixer(h, mix_gain, w_in, conv_w, conv_b, dt_bias, a_log, d_skip, norm_g, w_out, batch, seq):
    main = SSD_INNER + conv_w.shape[1]
    proj = fused_matmul(h, w_in[:, :main].astype(BF16), gain=mix_gain, out_dtype=BF16, tm=1024, tn=1024)
    dt_raw = fused_matmul(h, w_in[:, main:].astype(BF16), gain=mix_gain, out_dtype=F32, tm=1024, tn=128)
    xbc = ssd_pre(proj, conv_w, conv_b, seq)
    y_back = ssd_scan(xbc, dt_raw, dt_bias, a_log, batch, seq, reverse=True)
    y = ssd_scan(xbc, dt_raw, dt_bias, a_log, batch, seq, reverse=False,
                 y_back=y_back, proj=proj, d_skip=d_skip)
    return fused_matmul(y, w_out.astype(BF16), gain=norm_g, resid=h, out_dtype=F32, tm=512, tn=1024)


def _mlstm_scan_body(*refs, reverse, final):
    if final:
        (q_ref, k_ref, v_ref, gate_ref, gb_ref, hb_ref, o_ref, hn_ref,
         out_ref, c_state, n_state, m_state) = refs
    else:
        q_ref, k_ref, v_ref, gate_ref, gb_ref, out_ref, c_state, n_state, m_state = refs
    L = CHUNK
    dk, dv = MLSTM_QK_HEAD, MLSTM_V_HEAD
    edge = 0 if reverse else L - 1
    i_off = 2 * MLSTM_HEADS if reverse else 0
    f_off = i_off + MLSTM_HEADS

    @pl.when(pl.program_id(1) == 0)
    def _():
        c_state[...] = jnp.zeros_like(c_state)
        n_state[...] = jnp.zeros_like(n_state)
        m_state[...] = jnp.zeros_like(m_state)

    row = lax.broadcasted_iota(jnp.int32, (L, L), 0)
    col = lax.broadcasted_iota(jnp.int32, (L, L), 1)
    causal = (row <= col) if reverse else (row >= col)

    gates = gate_ref[...] + gb_ref[...]
    logf = -_softplus(-gates)
    fcum = _dot_f32(causal.astype(F32), logf)
    fcum_t = fcum.T
    gates_t = gates.T

    for h in range(MLSTM_HEADS):
        ci, cf = i_off + h, f_off + h
        q = q_ref[:, h * dk:(h + 1) * dk]
        k = k_ref[:, h * dk:(h + 1) * dk]
        v = v_ref[:, h * dv:(h + 1) * dv]
        f_col = fcum[:, cf:cf + 1]
        i_col = gates[:, ci:ci + 1]
        g_tot = fcum[edge:edge + 1, cf:cf + 1]
        m_prev = m_state[h]
        m_prev1 = m_prev[:, 0:1]
        c_prev = c_state[h]
        n_prev = n_state[h]

        dm = jnp.where(causal, _lane_bcast(f_col, L) - fcum_t[cf:cf + 1, :] + gates_t[ci:ci + 1, :], NEG_BIG)
        m_inter = f_col + m_prev1
        m_t = jnp.maximum(m_inter, jnp.max(dm, axis=-1, keepdims=True))
        s = jnp.exp(dm - m_t) * (_dot_nt(q, k) * (dk ** -0.5))
        inter = jnp.exp(m_inter - m_t)
        num = _dot(s.astype(BF16), v) + inter * _dot(q, c_prev.astype(BF16))
        den = (jnp.sum(s, axis=-1, keepdims=True)
               + inter * jnp.sum(q.astype(F32) * n_prev, axis=-1, keepdims=True))
        hout = num / jnp.maximum(jnp.abs(den), jnp.exp(-m_t))

        a = g_tot - f_col + i_col
        a_max = jnp.max(a, axis=0, keepdims=True)
        wk = k.astype(F32) * (jnp.exp(a - a_max) * (dk ** -0.5))
        kv_loc = _dot_tn(wk.astype(BF16), v)
        n_loc = jnp.sum(wk, axis=0, keepdims=True)
        m_new = jnp.maximum(g_tot + m_prev1, a_max)
        s_old = jnp.exp(g_tot + m_prev1 - m_new)
        s_new = jnp.exp(a_max - m_new)
        c_state[h] = s_old * c_prev + s_new * kv_loc
        n_state[h] = s_old * n_prev + s_new * n_loc
        m_state[h] = jnp.broadcast_to(m_new, m_prev.shape)

        if final:
            hs = hout + hb_ref[:, h * dv:(h + 1) * dv]
            ms = jnp.mean(hs * hs, axis=-1, keepdims=True)
            hs = hs * lax.rsqrt(ms + NORM_EPS) * hn_ref[:, h * dv:(h + 1) * dv]
            hs = hs * _sigmoid(o_ref[:, h * dv:(h + 1) * dv].astype(F32))
            out_ref[:, h * dv:(h + 1) * dv] = hs.astype(out_ref.dtype)
        else:
            out_ref[:, h * dv:(h + 1) * dv] = hout


def mlstm_scan(proj, gates, gate_bias, batch, seq, *, reverse, h_back=None, head_norm=None):
    final = h_back is not None
    total = proj.shape[0]
    nc = seq // CHUNK
    qk = MLSTM_HEADS * MLSTM_QK_HEAD
    vd = MLSTM_HEADS * MLSTM_V_HEAD

    def rows(b, c):
        return b * nc + (nc - 1 - c if reverse else c)

    in_specs = [
        pl.BlockSpec((CHUNK, qk), lambda b, c: (rows(b, c), 0)),
        pl.BlockSpec((CHUNK, qk), lambda b, c: (rows(b, c), 1)),
        pl.BlockSpec((CHUNK, vd), lambda b, c: (rows(b, c), 1)),
        pl.BlockSpec((CHUNK, 128), lambda b, c: (rows(b, c), 0)),
        pl.BlockSpec((1, 128), lambda b, c: (0, 0)),
    ]
    args = [proj, proj, proj, gates, gate_bias]
    if final:
        in_specs += [
            pl.BlockSpec((CHUNK, vd), lambda b, c: (rows(b, c), 0)),
            pl.BlockSpec((CHUNK, vd), lambda b, c: (rows(b, c), 2)),
            pl.BlockSpec((1, vd), lambda b, c: (0, 0)),
        ]
        args += [h_back, proj, head_norm.reshape(1, vd)]
    body = functools.partial(_mlstm_scan_body, reverse=reverse, final=final)
    return pl.pallas_call(
        body,
        grid=(batch, nc),
        in_specs=in_specs,
        out_specs=pl.BlockSpec((CHUNK, vd), lambda b, c: (rows(b, c), 0)),
        out_shape=jax.ShapeDtypeStruct((total, vd), BF16 if final else F32),
        scratch_shapes=[
            pltpu.VMEM((MLSTM_HEADS, MLSTM_QK_HEAD, MLSTM_V_HEAD), F32),
            pltpu.VMEM((MLSTM_HEADS, 1, MLSTM_QK_HEAD), F32),
            pltpu.VMEM((MLSTM_HEADS, 1, 128), F32),
        ],
        compiler_params=_params("parallel", "arbitrary"),
    )(*args)


def mlstm_mixer(h, mix_gain, w_in, gate_bias, head_norm, w_out, batch, seq):
    main = 2 * MLSTM_HEADS * MLSTM_QK_HEAD + 2 * MLSTM_HEADS * MLSTM_V_HEAD
    ngate = 4 * MLSTM_HEADS
    proj = fused_matmul(h, w_in[:, :main].astype(BF16), gain=mix_gain, out_dtype=BF16, tm=1024, tn=1024)
    w_gate = jnp.pad(w_in[:, main:], ((0, 0), (0, 128 - ngate))).astype(BF16)
    gates = fused_matmul(h, w_gate, gain=mix_gain, out_dtype=F32, tm=1024, tn=128)
    bias = jnp.pad(gate_bias.reshape(1, ngate), ((0, 0), (0, 128 - ngate)))
    h_back = mlstm_scan(proj, gates, bias, batch, seq, reverse=True)
    y = mlstm_scan(proj, gates, bias, batch, seq, reverse=False, h_back=h_back, head_norm=head_norm)
    return fused_matmul(y, w_out.astype(BF16), resid=h, out_dtype=F32, tm=1024, tn=1024)


def _rglru_body(*refs, reverse, final, tile_rows, seq):
    if final:
        (x_ref, prev_ref, next_ref, cw_ref, cb_ref, gw_ref, gb_ref, lam_ref, hb_ref, br_ref,
         o_ref, a_buf, b_buf, carry) = refs
    else:
        (x_ref, prev_ref, next_ref, cw_ref, cb_ref, gw_ref, gb_ref, lam_ref,
         o_ref, a_buf, b_buf, carry) = refs
    width = RG_BLOCKS * RG_BLOCK_DIM
    nt = seq // tile_rows
    t = pl.program_id(1)
    tile_idx = pl.program_id(0) * nt + (nt - 1 - t if reverse else t)

    @pl.when(t == 0)
    def _():
        carry[...] = jnp.zeros_like(carry)

    prev8, next8 = _halo_rows(prev_ref, next_ref, tile_idx, tile_rows, seq)
    xc = _dwconv_tile(x_ref[...].astype(F32), prev8, next8, cw_ref[...], 2) + cb_ref[...]
    for n in range(RG_BLOCKS):
        lo = n * RG_BLOCK_DIM
        xn = xc[:, lo:lo + RG_BLOCK_DIM]
        g = _dot(xn.astype(BF16), gw_ref[n]) + gb_ref[:, 2 * lo:2 * lo + 2 * RG_BLOCK_DIM]
        r = _sigmoid(g[:, :RG_BLOCK_DIM])
        i = _sigmoid(g[:, RG_BLOCK_DIM:])
        log_a = (-RG_C) * r * _softplus(-lam_ref[:, lo:lo + RG_BLOCK_DIM])
        a_buf[:, lo:lo + RG_BLOCK_DIM] = jnp.exp(log_a)
        b_buf[:, lo:lo + RG_BLOCK_DIM] = jnp.sqrt(1.0 - jnp.exp(2.0 * log_a)) * (i * xn)

    srow = lax.broadcasted_iota(jnp.int32, (8, width), 0)
    nslab = tile_rows // 8

    def slab(s, c):
        idx = nslab - 1 - s if reverse else s
        sl = pl.ds(pl.multiple_of(idx * 8, 8), 8)
        a = a_buf[sl, :]
        b = b_buf[sl, :]
        for step in (1, 2, 4):
            if reverse:
                valid = srow < 8 - step
                shift = 8 - step
            else:
                valid = srow >= step
                shift = step
            b = jnp.where(valid, a * pltpu.roll(b, shift, 0) + b, b)
            a = jnp.where(valid, a * pltpu.roll(a, shift, 0), a)
        hs = a * c + b
        b_buf[sl, :] = hs
        return hs[0:1, :] if reverse else hs[7:8, :]

    carry[...] = lax.fori_loop(0, nslab, slab, carry[...])
    hs = b_buf[...]
    if final:
        hs = (hs + hb_ref[...]) * jax.nn.gelu(br_ref[...].astype(F32))
    o_ref[...] = hs.astype(o_ref.dtype)


def rglru_scan(proj, conv_w, conv_b, gate_w, gate_b, lam, batch, seq, *, reverse, h_back=None,
               tile_rows=256):
    final = h_back is not None
    total = proj.shape[0]
    width = RG_BLOCKS * RG_BLOCK_DIM
    nt = seq // tile_rows

    def tile(b, t):
        return b * nt + (nt - 1 - t if reverse else t)

    per = tile_rows // HALO
    last = total // HALO - 1
    in_specs = [
        pl.BlockSpec((tile_rows, width), lambda b, t: (tile(b, t), 1)),
        pl.BlockSpec((HALO, width), lambda b, t: (jnp.maximum(tile(b, t) * per - 1, 0), 1)),
        pl.BlockSpec((HALO, width), lambda b, t: (jnp.minimum((tile(b, t) + 1) * per, last), 1)),
        pl.BlockSpec((4, width), lambda b, t: (0, 0)),
        pl.BlockSpec((1, width), lambda b, t: (0, 0)),
        pl.BlockSpec((RG_BLOCKS, RG_BLOCK_DIM, 2 * RG_BLOCK_DIM), lambda b, t: (0, 0, 0)),
        pl.BlockSpec((1, 2 * width), lambda b, t: (0, 0)),
        pl.BlockSpec((1, width), lambda b, t: (0, 0)),
    ]
    args = [proj, proj, proj, conv_w, conv_b.reshape(1, width), gate_w.astype(BF16),
            gate_b.reshape(1, 2 * width), lam.reshape(1, width)]
    if final:
        in_specs += [
            pl.BlockSpec((tile_rows, width), lambda b, t: (tile(b, t), 0)),
            pl.BlockSpec((tile_rows, width), lambda b, t: (tile(b, t), 0)),
        ]
        args += [h_back, proj]
    body = functools.partial(_rglru_body, reverse=reverse, final=final, tile_rows=tile_rows, seq=seq)
    return pl.pallas_call(
        body,
        grid=(batch, nt),
        in_specs=in_specs,
        out_specs=pl.BlockSpec((tile_rows, width), lambda b, t: (tile(b, t), 0)),
        out_shape=jax.ShapeDtypeStruct((total, width), BF16 if final else F32),
        scratch_shapes=[
            pltpu.VMEM((tile_rows, width), F32),
            pltpu.VMEM((tile_rows, width), F32),
            pltpu.VMEM((1, width), F32),
        ],
        compiler_params=_params("parallel", "arbitrary"),
    )(*args)


def rglru_mixer(h, mix_gain, w_in, conv_w, conv_b, gate_w, gate_b, lam, w_out, batch, seq):
    proj = fused_matmul(h, w_in.astype(BF16), gain=mix_gain, out_dtype=BF16, tm=1024, tn=1024)
    h_back = rglru_scan(proj, conv_w, conv_b, gate_w[1], gate_b[1], lam[1], batch, seq, reverse=True)
    y = rglru_scan(proj, conv_w, conv_b, gate_w[0], gate_b[0], lam[0], batch, seq, reverse=False,
                   h_back=h_back)
    return fused_matmul(y, w_out.astype(BF16), resid=h, out_dtype=F32, tm=1024, tn=1024)


def _xattn_body(q_ref, kv_ref, o_ref):
    d = XA_HEAD_DIM
    for h in range(XA_HEADS):
        q = q_ref[:, h * d:(h + 1) * d]
        k = kv_ref[0, :, h * d:(h + 1) * d]
        v = kv_ref[0, :, (XA_HEADS + h) * d:(XA_HEADS + h + 1) * d]
        s = _dot_nt(q, k) * (d ** -0.5)
        e = jnp.exp(s - jnp.max(s, axis=-1, keepdims=True))
        p = e / jnp.sum(e, axis=-1, keepdims=True)
        o_ref[:, h * d:(h + 1) * d] = _dot(p.astype(BF16), v).astype(o_ref.dtype)


def xattn_core(q, kv, seq, *, tm=512):
    total, dm = q.shape
    n_mem = kv.shape[1]
    per_seq = seq // tm
    return pl.pallas_call(
        _xattn_body,
        grid=(total // tm,),
        in_specs=[
            pl.BlockSpec((tm, dm), lambda i: (i, 0)),
            pl.BlockSpec((1, n_mem, 2 * dm), lambda i: (i // per_seq, 0, 0)),
        ],
        out_specs=pl.BlockSpec((tm, dm), lambda i: (i, 0)),
        out_shape=jax.ShapeDtypeStruct((total, dm), BF16),
        compiler_params=_params("parallel"),
    )(q, kv)


def _ffn_down_body(g_ref, prev_ref, next_ref, v_ref, cw_ref, cb_ref, w_ref, r_ref, o_ref, *, tile_rows, seq):
    k = pl.program_id(1)
    prev8, next8 = _halo_rows(prev_ref, next_ref, pl.program_id(0), tile_rows, seq)
    gate = _dwconv_tile(g_ref[...].astype(F32), prev8, next8, cw_ref[...], 1) + cb_ref[...]
    act = (_silu(gate) * v_ref[...].astype(F32)).astype(BF16)
    part = _dot(act, w_ref[...])

    @pl.when(k == 0)
    def _():
        o_ref[...] = r_ref[...] + part

    @pl.when(k != 0)
    def _():
        o_ref[...] = o_ref[...] + part


def ffn_down(up, conv_w, conv_b, w_down, resid, seq, *, tm=512, tk=1408):
    total = up.shape[0]
    f, dm = w_down.shape
    nk = f // tk
    prev_spec, next_spec = _halo_specs(tm, total, tk, lambda i, k: k)
    body = functools.partial(_ffn_down_body, tile_rows=tm, seq=seq)
    return pl.pallas_call(
        body,
        grid=(total // tm, nk),
        in_specs=[
            pl.BlockSpec((tm, tk), lambda i, k: (i, k)),
            prev_spec, next_spec,
            pl.BlockSpec((tm, tk), lambda i, k: (i, nk + k)),
            pl.BlockSpec((3, tk), lambda i, k: (0, k)),
            pl.BlockSpec((1, tk), lambda i, k: (0, k)),
            pl.BlockSpec((tk, dm), lambda i, k: (k, 0)),
            pl.BlockSpec((tm, dm), lambda i, k: (i, 0)),
        ],
        out_specs=pl.BlockSpec((tm, dm), lambda i, k: (i, 0)),
        out_shape=jax.ShapeDtypeStruct((total, dm), F32),
        compiler_params=_params("parallel", "arbitrary"),
    )(up, up, up, up, conv_w, conv_b.reshape(1, f), w_down, resid)


def _rmsnorm_body(a_ref, g_ref, o_ref):
    a = a_ref[...]
    ms = jnp.mean(a * a, axis=-1, keepdims=True)
    o_ref[...] = a * lax.rsqrt(ms + NORM_EPS) * g_ref[...]


def rmsnorm_rows(a, gain, *, tm=512):
    total, dm = a.shape
    return pl.pallas_call(
        _rmsnorm_body,
        grid=(total // tm,),
        in_specs=[pl.BlockSpec((tm, dm), lambda i: (i, 0)), pl.BlockSpec((1, dm), lambda i: (0, 0))],
        out_specs=pl.BlockSpec((tm, dm), lambda i: (i, 0)),
        out_shape=jax.ShapeDtypeStruct((total, dm), F32),
        compiler_params=_params("parallel"),
    )(a, gain.reshape(1, dm))


def kernel(x, mem, mem_norm, mix_norm, xattn_norm, xattn_wq, xattn_wkv, xattn_wo, ffn_norm, ffn_w_up, ffn_conv_w, ffn_conv_b, ffn_w_down, ssd_w_in, ssd_conv_w, ssd_conv_b, ssd_dt_bias, ssd_a_log, ssd_d_skip, ssd_norm, ssd_w_out, mlstm_w_in, mlstm_gate_bias, mlstm_head_norm, mlstm_w_out, rglru_w_in, rglru_conv_w, rglru_conv_b, rglru_gate_w, rglru_gate_b, rglru_lambda, rglru_w_out, final_norm):
    batch, seq, dm = x.shape
    n_mem = mem.shape[1]
    depth = mix_norm.shape[0]
    h = x.reshape(batch * seq, dm)
    mem2 = mem.reshape(batch * n_mem, dm)
    for i in range(depth):
        kind, j = i % 3, i // 3
        if kind == 0:
            h = ssd_mixer(h, mix_norm[i], ssd_w_in[j], ssd_conv_w[j], ssd_conv_b[j], ssd_dt_bias[j],
                          ssd_a_log[j], ssd_d_skip[j], ssd_norm[j], ssd_w_out[j], batch, seq)
        elif kind == 1:
            h = mlstm_mixer(h, mix_norm[i], mlstm_w_in[j], mlstm_gate_bias[j], mlstm_head_norm[j],
                            mlstm_w_out[j], batch, seq)
        else:
            h = rglru_mixer(h, mix_norm[i], rglru_w_in[j], rglru_conv_w[j], rglru_conv_b[j],
                            rglru_gate_w[j], rglru_gate_b[j], rglru_lambda[j], rglru_w_out[j], batch, seq)
        q = fused_matmul(h, xattn_wq[i].astype(BF16), gain=xattn_norm[i], out_dtype=BF16, tm=1024, tn=1024)
        kv = fused_matmul(mem2, xattn_wkv[i].astype(BF16), gain=mem_norm, out_dtype=BF16, tm=512, tn=1024)
        att = xattn_core(q, kv.reshape(batch, n_mem, 2 * dm), seq)
        h = fused_matmul(att, xattn_wo[i].astype(BF16), resid=h, out_dtype=F32, tm=1024, tn=1024)
        up = fused_matmul(h, ffn_w_up[i].astype(BF16), gain=ffn_norm[i], out_dtype=BF16, tm=1024, tn=1408)
        h = ffn_down(up, ffn_conv_w[i], ffn_conv_b[i], ffn_w_down[i].astype(BF16), h, seq)
    return rmsnorm_rows(h, final_norm).reshape(batch, seq, dm)
```

```python
import functools
import math

import jax
import jax.numpy as jnp
from jax import lax
from jax.experimental import pallas as pl
from jax.experimental.pallas import tpu as pltpu

F32 = jnp.float32
BF16 = jnp.bfloat16

NORM_EPS = 1e-6
CHUNK = 128
HALO = 16
V7X_VMEM_LIMIT = 56 * 1024 * 1024
NEG_BIG = -1e30
LOG2_E = 1.4426950408889634

SSD_HEADS = 64
SSD_HEADDIM = 64
SSD_STATE = 128
SSD_GROUPS = 8
SSD_INNER = SSD_HEADS * SSD_HEADDIM
MLSTM_HEADS = 8
MLSTM_QK_HEAD = 128
MLSTM_V_HEAD = 256
RG_BLOCKS = 8
RG_BLOCK_DIM = 256
RG_C = 8.0
XA_HEADS = 4
XA_HEAD_DIM = 512


def _params(*sem):
    return pltpu.CompilerParams(dimension_semantics=sem, vmem_limit_bytes=V7X_VMEM_LIMIT)


def _sigmoid(x):
    return 1.0 / (1.0 + jnp.exp(-x))


def _softplus(x):
    return jnp.maximum(x, 0.0) + jnp.log(1.0 + jnp.exp(-jnp.abs(x)))


def _silu(x):
    return x * _sigmoid(x)


def _dot(a, b):
    return jnp.dot(a, b, preferred_element_type=F32)


def _dot_f32(a, b):
    return jnp.dot(a, b, preferred_element_type=F32, precision=lax.Precision.HIGHEST)


def _dot_nt(a, b):
    return lax.dot_general(a, b, (((1,), (1,)), ((), ())), preferred_element_type=F32)


def _dot_tn(a, b):
    return lax.dot_general(a, b, (((0,), (0,)), ((), ())), preferred_element_type=F32)


def _lane_bcast(col, width):
    return jnp.broadcast_to(col, (col.shape[0], width))


def _mm_body(*refs, has_gain, has_resid, stage_lhs):
    it = iter(refs)
    a_ref = next(it)
    g_ref = next(it) if has_gain else None
    w_ref = next(it)
    r_ref = next(it) if has_resid else None
    o_ref = next(it)
    lhs_ref = next(it) if stage_lhs else None

    if stage_lhs:
        @pl.when(pl.program_id(1) == 0)
        def _():
            a = a_ref[...].astype(F32)
            if has_gain:
                ms = jnp.mean(a * a, axis=-1, keepdims=True)
                a = a * lax.rsqrt(ms + NORM_EPS) * g_ref[...]
            lhs_ref[...] = a.astype(BF16)
        lhs = lhs_ref[...]
    else:
        lhs = a_ref[...]
    acc = _dot(lhs, w_ref[...])
    if has_resid:
        acc = acc + r_ref[...]
    o_ref[...] = acc.astype(o_ref.dtype)


def fused_matmul(a, w, *, gain=None, resid=None, out_dtype, tm, tn):
    m, k = a.shape
    n = w.shape[1]
    tm = min(tm, m)
    tn = min(tn, n)
    assert m % tm == 0 and n % tn == 0, (m, n, tm, tn)
    has_gain = gain is not None
    has_resid = resid is not None
    stage_lhs = has_gain or a.dtype != BF16
    in_specs = [pl.BlockSpec((tm, k), lambda i, j: (i, 0))]
    args = [a]
    if has_gain:
        in_specs.append(pl.BlockSpec((1, k), lambda i, j: (0, 0)))
        args.append(gain.reshape(1, k).astype(F32))
    in_specs.append(pl.BlockSpec((k, tn), lambda i, j: (0, j)))
    args.append(w)
    if has_resid:
        in_specs.append(pl.BlockSpec((tm, tn), lambda i, j: (i, j)))
        args.append(resid)
    scratch = [pltpu.VMEM((tm, k), BF16)] if stage_lhs else []
    body = functools.partial(_mm_body, has_gain=has_gain, has_resid=has_resid, stage_lhs=stage_lhs)
    return pl.pallas_call(
        body,
        grid=(m // tm, n // tn),
        in_specs=in_specs,
        out_specs=pl.BlockSpec((tm, tn), lambda i, j: (i, j)),
        out_shape=jax.ShapeDtypeStruct((m, n), out_dtype),
        scratch_shapes=scratch,
        compiler_params=_params("parallel", "arbitrary"),
    )(*args)


def _dwconv_tile(x, prev8, next8, w, left):
    rows = x.shape[0]
    width = w.shape[0]

    def taps(v):
        nrow = v.shape[0]
        acc = None
        for t in range(width):
            shift = (left - t) % nrow
            src = v if shift == 0 else pltpu.roll(v, shift, 0)
            term = src * w[t:t + 1, :]
            acc = term if acc is None else acc + term
        return acc

    body = taps(x)
    top = taps(jnp.concatenate([prev8, x[0:24]], axis=0))[8:16]
    bot = taps(jnp.concatenate([x[rows - 24:rows], next8], axis=0))[16:24]
    return jnp.concatenate([top, body[8:rows - 8], bot], axis=0)


def _halo_rows(prev_ref, next_ref, tile_idx, tile_rows, seq):
    start = tile_idx * tile_rows
    keep_prev = (start % seq != 0).astype(F32)
    keep_next = ((start + tile_rows) % seq != 0).astype(F32)
    prev8 = prev_ref[HALO - 8:HALO, :].astype(F32) * keep_prev
    next8 = next_ref[0:8, :].astype(F32) * keep_next
    return prev8, next8


def _halo_specs(tile_rows, total_rows, width, col_of):
    per = tile_rows // HALO
    last = total_rows // HALO - 1
    prev = pl.BlockSpec((HALO, width), lambda *g: (jnp.maximum(g[0] * per - 1, 0), col_of(*g)))
    nxt = pl.BlockSpec((HALO, width), lambda *g: (jnp.minimum((g[0] + 1) * per, last), col_of(*g)))
    return prev, nxt


def _ssd_pre_body(x_ref, prev_ref, next_ref, w_ref, b_ref, o_ref, *, tile_rows, seq):
    prev8, next8 = _halo_rows(prev_ref, next_ref, pl.program_id(0), tile_rows, seq)
    y = _dwconv_tile(x_ref[...].astype(F32), prev8, next8, w_ref[...], 2) + b_ref[...]
    o_ref[...] = _silu(y).astype(o_ref.dtype)


def ssd_pre(proj, conv_w, conv_b, seq, *, tile_rows=512, tc=1024):
    total = proj.shape[0]
    cdim = conv_w.shape[1]
    col0 = SSD_INNER // tc
    prev_spec, next_spec = _halo_specs(tile_rows, total, tc, lambda i, c: col0 + c)
    body = functools.partial(_ssd_pre_body, tile_rows=tile_rows, seq=seq)
    return pl.pallas_call(
        body,
        grid=(total // tile_rows, cdim // tc),
        in_specs=[
            pl.BlockSpec((tile_rows, tc), lambda i, c: (i, col0 + c)),
            prev_spec, next_spec,
            pl.BlockSpec((4, tc), lambda i, c: (0, c)),
            pl.BlockSpec((1, tc), lambda i, c: (0, c)),
        ],
        out_specs=pl.BlockSpec((tile_rows, tc), lambda i, c: (i, c)),
        out_shape=jax.ShapeDtypeStruct((total, cdim), BF16),
        compiler_params=_params("parallel", "parallel"),
    )(proj, proj, proj, conv_w, conv_b.reshape(1, cdim))


def _ssd_scan_body(*refs, reverse, final):
    if final:
        (x_ref, b_ref, c_ref, dt_ref, dtb_ref, alog_ref, yb_ref, z_ref, dskip_ref,
         o_ref, state_ref) = refs
    else:
        x_ref, b_ref, c_ref, dt_ref, dtb_ref, alog_ref, o_ref, state_ref = refs
    L = CHUNK
    P2 = 2 * SSD_HEADDIM
    GW = SSD_HEADS // SSD_GROUPS * SSD_HEADDIM
    off = SSD_HEADS if reverse else 0
    edge = 0 if reverse else L - 1

    @pl.when(pl.program_id(1) == 0)
    def _():
        state_ref[...] = jnp.zeros_like(state_ref)

    row = lax.broadcasted_iota(jnp.int32, (L, L), 0)
    col = lax.broadcasted_iota(jnp.int32, (L, L), 1)
    causal = (row <= col) if reverse else (row >= col)
    lane_lo = lax.broadcasted_iota(jnp.int32, (L, P2), 1) < SSD_HEADDIM

    dt = _softplus(dt_ref[...] + dtb_ref[...])
    la = dt * (-jnp.exp(alog_ref[...]))
    cum = _dot_f32(causal.astype(F32), la) * LOG2_E
    src_t = (cum - jnp.log2(dt)).T
    total_row = cum[edge:edge + 1, :]
    w_t = (jnp.exp2(total_row - cum) * dt).T

    for g in range(SSD_GROUPS):
        bg = b_ref[:, g * SSD_STATE:(g + 1) * SSD_STATE]
        cg = c_ref[:, g * SSD_STATE:(g + 1) * SSD_STATE]
        cb = _dot_nt(cg, bg)
        bg_t = bg.astype(F32).T
        y_prev = _dot(cg, state_ref[:, g * GW:(g + 1) * GW].astype(BF16))
        for j in range(GW // P2):
            lo = g * GW + j * P2
            h1 = off + (lo // SSD_HEADDIM)
            x2 = x_ref[:, lo:lo + P2]
            e_cols, m_parts, z_parts = [], [], []
            for h in (h1, h1 + 1):
                e_col = _lane_bcast(cum[:, h:h + 1], L)
                diff = jnp.where(causal, e_col - src_t[h:h + 1, :], NEG_BIG)
                m_parts.append((cb * jnp.exp2(diff)).astype(BF16))
                z_parts.append((bg_t * w_t[h:h + 1, :]).astype(BF16))
                e_cols.append(e_col)
            lhs = jnp.concatenate([jnp.concatenate(m_parts, axis=1), jnp.concatenate(z_parts, axis=1)], axis=0)
            zero = jnp.zeros_like(x2)
            rhs = jnp.concatenate([jnp.where(lane_lo, x2, zero), jnp.where(lane_lo, zero, x2)], axis=0)
            res = _dot(lhs, rhs)
            e_pair = jnp.where(lane_lo, e_cols[0], e_cols[1])
            y = res[0:L] + jnp.exp2(e_pair) * y_prev[:, j * P2:(j + 1) * P2]
            decay = jnp.exp2(e_pair[edge:edge + 1, :])
            state_ref[:, lo:lo + P2] = state_ref[:, lo:lo + P2] * decay + res[L:2 * L]
            if final:
                y = y + yb_ref[:, lo:lo + P2] + x2.astype(F32) * dskip_ref[:, lo:lo + P2]
                y = y * _silu(z_ref[:, lo:lo + P2].astype(F32))
            o_ref[:, lo:lo + P2] = y.astype(o_ref.dtype)


def ssd_scan(xbc, dt_raw, dt_bias, a_log, batch, seq, *, reverse, y_back=None, proj=None, d_skip=None):
    final = y_back is not None
    total = xbc.shape[0]
    nc = seq // CHUNK
    bw = SSD_GROUPS * SSD_STATE

    def rows(b, c):
        return b * nc + (nc - 1 - c if reverse else c)

    in_specs = [
        pl.BlockSpec((CHUNK, SSD_INNER), lambda b, c: (rows(b, c), 0)),
        pl.BlockSpec((CHUNK, bw), lambda b, c: (rows(b, c), SSD_INNER // bw)),
        pl.BlockSpec((CHUNK, bw), lambda b, c: (rows(b, c), SSD_INNER // bw + 1)),
        pl.BlockSpec((CHUNK, 2 * SSD_HEADS), lambda b, c: (rows(b, c), 0)),
        pl.BlockSpec((1, 2 * SSD_HEADS), lambda b, c: (0, 0)),
        pl.BlockSpec((1, 2 * SSD_HEADS), lambda b, c: (0, 0)),
    ]
    args = [xbc, xbc, xbc, dt_raw, dt_bias.reshape(1, -1), a_log.reshape(1, -1)]
    if final:
        in_specs += [
            pl.BlockSpec((CHUNK, SSD_INNER), lambda b, c: (rows(b, c), 0)),
            pl.BlockSpec((CHUNK, SSD_INNER), lambda b, c: (rows(b, c), 0)),
            pl.BlockSpec((1, SSD_INNER), lambda b, c: (0, 0)),
        ]
        args += [y_back, proj, jnp.repeat(d_skip, SSD_HEADDIM).reshape(1, SSD_INNER)]
    body = functools.partial(_ssd_scan_body, reverse=reverse, final=final)
    return pl.pallas_call(
        body,
        grid=(batch, nc),
        in_specs=in_specs,
        out_specs=pl.BlockSpec((CHUNK, SSD_INNER), lambda b, c: (rows(b, c), 0)),
        out_shape=jax.ShapeDtypeStruct((total, SSD_INNER), BF16 if final else F32),
        scratch_shapes=[pltpu.VMEM((SSD_STATE, SSD_INNER), F32)],
        compiler_params=_params("parallel", "arbitrary"),
    )(*args)


def ssd_mixer(h, mix_gain, w_in, conv_w, conv_b, dt_bias, a_log, d_skip, norm_g, w_out, batch, seq):
    main = SSD_INNER + conv_w.shape[1]
    proj = fused_matmul(h, w_in[:, :main].astype(BF16), gain=mix_gain, out_dtype=BF16, tm=1024, tn=1024)
    dt_raw = fused_matmul(h, w_in[:, main:].astype(BF16), gain=mix_gain, out_dtype=F32, tm=1024, tn=128)
    xbc = ssd_pre(proj, conv_w, conv_b, seq)
    y_back = ssd_scan(xbc, dt_raw, dt_bias, a_log, batch, seq, reverse=True)
    y = ssd_scan(xbc, dt_raw, dt_bias, a_log, batch, seq, reverse=False,
                 y_back=y_back, proj=proj, d_skip=d_skip)
    return fused_matmul(y, w_out.astype(BF16), gain=norm_g, resid=h, out_dtype=F32, tm=512, tn=1024)


def _running_max(u, reverse):
    nrow = u.shape[0]
    rowi = lax.broadcasted_iota(jnp.int32, u.shape, 0)
    step = 1
    while step < nrow:
        if reverse:
            shifted = jnp.where(rowi < nrow - step, pltpu.roll(u, nrow - step, 0), NEG_BIG)
        else:
            shifted = jnp.where(rowi >= step, pltpu.roll(u, step, 0), NEG_BIG)
        u = jnp.maximum(u, shifted)
        step *= 2
    return u


def _mlstm_scan_body(*refs, reverse, final):
    if final:
        q_ref, k_ref, v_ref, gate_ref, gb_ref, hb_ref, o_ref, hn_ref, out_ref, c_state, m_state = refs
    else:
        q_ref, k_ref, v_ref, gate_ref, gb_ref, out_ref, c_state, m_state = refs
    L = CHUNK
    dk, dv = MLSTM_QK_HEAD, MLSTM_V_HEAD
    edge = 0 if reverse else L - 1
    f_off = (3 if reverse else 1) * MLSTM_HEADS
    scale = dk ** -0.5

    @pl.when(pl.program_id(1) == 0)
    def _():
        c_state[...] = jnp.zeros_like(c_state)
        m_state[...] = jnp.zeros_like(m_state)

    row = lax.broadcasted_iota(jnp.int32, (L, L), 0)
    col = lax.broadcasted_iota(jnp.int32, (L, L), 1)
    causal = (row <= col) if reverse else (row >= col)

    gates = gate_ref[...] + gb_ref[...]
    fcum = _dot_f32(causal.astype(F32), -_softplus(-gates))
    u = pltpu.roll(gates, MLSTM_HEADS, 1) - fcum
    m_prev = m_state[...]
    g_tot = fcum[edge:edge + 1, :]
    m_inter = fcum + m_prev
    m_t = jnp.maximum(m_inter, fcum + _running_max(u, reverse))
    colv = fcum - m_t + math.log(scale)
    inter = jnp.exp(m_inter - m_t)
    floor = jnp.exp(-m_t)
    a = g_tot + u
    a_max = jnp.max(a, axis=0, keepdims=True)
    w_t = (jnp.exp(a - a_max) * scale).T
    u_t = u.T
    m_new = jnp.maximum(g_tot + m_prev, a_max)
    s_old = jnp.exp(g_tot + m_prev - m_new)
    s_new = jnp.exp(a_max - m_new)
    m_state[...] = m_new
    ones = jnp.ones((L, 128), BF16)

    for h in range(MLSTM_HEADS):
        cf = f_off + h
        q = q_ref[:, h * dk:(h + 1) * dk]
        k = k_ref[:, h * dk:(h + 1) * dk]
        v_ext = jnp.concatenate([v_ref[:, h * dv:(h + 1) * dv], ones], axis=1)
        c_prev = c_state[h]

        ex = _lane_bcast(colv[:, cf:cf + 1], L) + u_t[cf:cf + 1, :]
        s = (jnp.exp(jnp.where(causal, ex, NEG_BIG)) * _dot_nt(q, k)).astype(BF16)
        inter_b = _lane_bcast(inter[:, cf:cf + 1], 128)
        tot = _dot(s, v_ext) + jnp.concatenate([inter_b] * 3, axis=1) * _dot(q, c_prev.astype(BF16))
        r = 1.0 / jnp.maximum(jnp.abs(tot[:, dv:]), _lane_bcast(floor[:, cf:cf + 1], 128))
        hout = tot[:, :dv] * jnp.concatenate([r, r], axis=1)

        wk_t = (k.astype(F32).T * w_t[cf:cf + 1, :]).astype(BF16)
        c_state[h] = s_old[:, cf:cf + 1] * c_prev + s_new[:, cf:cf + 1] * _dot(wk_t, v_ext)

        if final:
            hs = hout + hb_ref[:, h * dv:(h + 1) * dv]
            ms = jnp.mean(hs * hs, axis=-1, keepdims=True)
            hs = hs * lax.rsqrt(ms + NORM_EPS) * hn_ref[:, h * dv:(h + 1) * dv]
            hs = hs * _sigmoid(o_ref[:, h * dv:(h + 1) * dv].astype(F32))
            out_ref[:, h * dv:(h + 1) * dv] = hs.astype(out_ref.dtype)
        else:
            out_ref[:, h * dv:(h + 1) * dv] = hout


def mlstm_scan(proj, gates, gate_bias, batch, seq, *, reverse, h_back=None, head_norm=None):
    final = h_back is not None
    total = proj.shape[0]
    nc = seq // CHUNK
    qk = MLSTM_HEADS * MLSTM_QK_HEAD
    vd = MLSTM_HEADS * MLSTM_V_HEAD

    def rows(b, c):
        return b * nc + (nc - 1 - c if reverse else c)

    in_specs = [
        pl.BlockSpec((CHUNK, qk), lambda b, c: (rows(b, c), 0)),
        pl.BlockSpec((CHUNK, qk), lambda b, c: (rows(b, c), 1)),
        pl.BlockSpec((CHUNK, vd), lambda b, c: (rows(b, c), 1)),
        pl.BlockSpec((CHUNK, 128), lambda b, c: (rows(b, c), 0)),
        pl.BlockSpec((1, 128), lambda b, c: (0, 0)),
    ]
    args = [proj, proj, proj, gates, gate_bias]
    if final:
        in_specs += [
            pl.BlockSpec((CHUNK, vd), lambda b, c: (rows(b, c), 0)),
            pl.BlockSpec((CHUNK, vd), lambda b, c: (rows(b, c), 2)),
            pl.BlockSpec((1, vd), lambda b, c: (0, 0)),
        ]
        args += [h_back, proj, head_norm.reshape(1, vd)]
    body = functools.partial(_mlstm_scan_body, reverse=reverse, final=final)
    return pl.pallas_call(
        body,
        grid=(batch, nc),
        in_specs=in_specs,
        out_specs=pl.BlockSpec((CHUNK, vd), lambda b, c: (rows(b, c), 0)),
        out_shape=jax.ShapeDtypeStruct((total, vd), BF16 if final else F32),
        scratch_shapes=[
            pltpu.VMEM((MLSTM_HEADS, MLSTM_QK_HEAD, MLSTM_V_HEAD + 128), F32),
            pltpu.VMEM((1, 128), F32),
        ],
        compiler_params=_params("parallel", "arbitrary"),
    )(*args)


def mlstm_mixer(h, mix_gain, w_in, gate_bias, head_norm, w_out, batch, seq):
    main = 2 * MLSTM_HEADS * MLSTM_QK_HEAD + 2 * MLSTM_HEADS * MLSTM_V_HEAD
    ngate = 4 * MLSTM_HEADS
    proj = fused_matmul(h, w_in[:, :main].astype(BF16), gain=mix_gain, out_dtype=BF16, tm=1024, tn=1024)
    w_gate = jnp.pad(w_in[:, main:], ((0, 0), (0, 128 - ngate))).astype(BF16)
    gates = fused_matmul(h, w_gate, gain=mix_gain, out_dtype=F32, tm=1024, tn=128)
    bias = jnp.pad(gate_bias.reshape(1, ngate), ((0, 0), (0, 128 - ngate)))
    h_back = mlstm_scan(proj, gates, bias, batch, seq, reverse=True)
    y = mlstm_scan(proj, gates, bias, batch, seq, reverse=False, h_back=h_back, head_norm=head_norm)
    return fused_matmul(y, w_out.astype(BF16), resid=h, out_dtype=F32, tm=1024, tn=1024)


def _rglru_body(*refs, reverse, final, tile_rows, seq):
    if final:
        (x_ref, prev_ref, next_ref, cw_ref, cb_ref, gw_ref, gb_ref, lam_ref, hb_ref, br_ref,
         o_ref, a_buf, b_buf, carry) = refs
    else:
        (x_ref, prev_ref, next_ref, cw_ref, cb_ref, gw_ref, gb_ref, lam_ref,
         o_ref, a_buf, b_buf, carry) = refs
    width = RG_BLOCKS * RG_BLOCK_DIM
    nt = seq // tile_rows
    t = pl.program_id(1)
    tile_idx = pl.program_id(0) * nt + (nt - 1 - t if reverse else t)

    @pl.when(t == 0)
    def _():
        carry[...] = jnp.zeros_like(carry)

    prev8, next8 = _halo_rows(prev_ref, next_ref, tile_idx, tile_rows, seq)
    xc = _dwconv_tile(x_ref[...].astype(F32), prev8, next8, cw_ref[...], 2) + cb_ref[...]
    for n in range(RG_BLOCKS):
        lo = n * RG_BLOCK_DIM
        xn = xc[:, lo:lo + RG_BLOCK_DIM]
        g = _dot(xn.astype(BF16), gw_ref[n]) + gb_ref[:, 2 * lo:2 * lo + 2 * RG_BLOCK_DIM]
        r = _sigmoid(g[:, :RG_BLOCK_DIM])
        i = _sigmoid(g[:, RG_BLOCK_DIM:])
        log_a = (-RG_C) * r * _softplus(-lam_ref[:, lo:lo + RG_BLOCK_DIM])
        a_buf[:, lo:lo + RG_BLOCK_DIM] = jnp.exp(log_a)
        b_buf[:, lo:lo + RG_BLOCK_DIM] = jnp.sqrt(1.0 - jnp.exp(2.0 * log_a)) * (i * xn)

    srow = lax.broadcasted_iota(jnp.int32, (8, width), 0)
    nslab = tile_rows // 8

    def slab(s, c):
        idx = nslab - 1 - s if reverse else s
        sl = pl.ds(pl.multiple_of(idx * 8, 8), 8)
        a = a_buf[sl, :]
        b = b_buf[sl, :]
        for step in (1, 2, 4):
            if reverse:
                valid = srow < 8 - step
                shift = 8 - step
            else:
                valid = srow >= step
                shift = step
            b = jnp.where(valid, a * pltpu.roll(b, shift, 0) + b, b)
            a = jnp.where(valid, a * pltpu.roll(a, shift, 0), a)
        hs = a * c + b
        b_buf[sl, :] = hs
        return hs[0:1, :] if reverse else hs[7:8, :]

    carry[...] = lax.fori_loop(0, nslab, slab, carry[...])
    hs = b_buf[...]
    if final:
        hs = (hs + hb_ref[...]) * jax.nn.gelu(br_ref[...].astype(F32))
    o_ref[...] = hs.astype(o_ref.dtype)


def rglru_scan(proj, conv_w, conv_b, gate_w, gate_b, lam, batch, seq, *, reverse, h_back=None,
               tile_rows=256):
    final = h_back is not None
    total = proj.shape[0]
    width = RG_BLOCKS * RG_BLOCK_DIM
    nt = seq // tile_rows

    def tile(b, t):
        return b * nt + (nt - 1 - t if reverse else t)

    per = tile_rows // HALO
    last = total // HALO - 1
    in_specs = [
        pl.BlockSpec((tile_rows, width), lambda b, t: (tile(b, t), 1)),
        pl.BlockSpec((HALO, width), lambda b, t: (jnp.maximum(tile(b, t) * per - 1, 0), 1)),
        pl.BlockSpec((HALO, width), lambda b, t: (jnp.minimum((tile(b, t) + 1) * per, last), 1)),
        pl.BlockSpec((4, width), lambda b, t: (0, 0)),
        pl.BlockSpec((1, width), lambda b, t: (0, 0)),
        pl.BlockSpec((RG_BLOCKS, RG_BLOCK_DIM, 2 * RG_BLOCK_DIM), lambda b, t: (0, 0, 0)),
        pl.BlockSpec((1, 2 * width), lambda b, t: (0, 0)),
        pl.BlockSpec((1, width), lambda b, t: (0, 0)),
    ]
    args = [proj, proj, proj, conv_w, conv_b.reshape(1, width), gate_w.astype(BF16),
            gate_b.reshape(1, 2 * width), lam.reshape(1, width)]
    if final:
        in_specs += [
            pl.BlockSpec((tile_rows, width), lambda b, t: (tile(b, t), 0)),
            pl.BlockSpec((tile_rows, width), lambda b, t: (tile(b, t), 0)),
        ]
        args += [h_back, proj]
    body = functools.partial(_rglru_body, reverse=reverse, final=final, tile_rows=tile_rows, seq=seq)
    return pl.pallas_call(
        body,
        grid=(batch, nt),
        in_specs=in_specs,
        out_specs=pl.BlockSpec((tile_rows, width), lambda b, t: (tile(b, t), 0)),
        out_shape=jax.ShapeDtypeStruct((total, width), BF16 if final else F32),
        scratch_shapes=[
            pltpu.VMEM((tile_rows, width), F32),
            pltpu.VMEM((tile_rows, width), F32),
            pltpu.VMEM((1, width), F32),
        ],
        compiler_params=_params("parallel", "arbitrary"),
    )(*args)


def rglru_mixer(h, mix_gain, w_in, conv_w, conv_b, gate_w, gate_b, lam, w_out, batch, seq):
    proj = fused_matmul(h, w_in.astype(BF16), gain=mix_gain, out_dtype=BF16, tm=1024, tn=1024)
    h_back = rglru_scan(proj, conv_w, conv_b, gate_w[1], gate_b[1], lam[1], batch, seq, reverse=True)
    y = rglru_scan(proj, conv_w, conv_b, gate_w[0], gate_b[0], lam[0], batch, seq, reverse=False,
                   h_back=h_back)
    return fused_matmul(y, w_out.astype(BF16), resid=h, out_dtype=F32, tm=1024, tn=1024)


def _xattn_body(q_ref, kv_ref, o_ref):
    d = XA_HEAD_DIM
    for h in range(XA_HEADS):
        q = q_ref[:, h * d:(h + 1) * d]
        k = kv_ref[0, :, h * d:(h + 1) * d]
        v = kv_ref[0, :, (XA_HEADS + h) * d:(XA_HEADS + h + 1) * d]
        s = _dot_nt(q, k) * (d ** -0.5)
        e = jnp.exp(s - jnp.max(s, axis=-1, keepdims=True))
        p = e / jnp.sum(e, axis=-1, keepdims=True)
        o_ref[:, h * d:(h + 1) * d] = _dot(p.astype(BF16), v).astype(o_ref.dtype)


def xattn_core(q, kv, seq, *, tm=512):
    total, dm = q.shape
    n_mem = kv.shape[1]
    per_seq = seq // tm
    return pl.pallas_call(
        _xattn_body,
        grid=(total // tm,),
        in_specs=[
            pl.BlockSpec((tm, dm), lambda i: (i, 0)),
            pl.BlockSpec((1, n_mem, 2 * dm), lambda i: (i // per_seq, 0, 0)),
        ],
        out_specs=pl.BlockSpec((tm, dm), lambda i: (i, 0)),
        out_shape=jax.ShapeDtypeStruct((total, dm), BF16),
        compiler_params=_params("parallel"),
    )(q, kv)


def _ffn_down_body(g_ref, prev_ref, next_ref, v_ref, cw_ref, cb_ref, w_ref, r_ref, o_ref, *, tile_rows, seq):
    k = pl.program_id(1)

    def partial_product():
        prev8, next8 = _halo_rows(prev_ref, next_ref, pl.program_id(0), tile_rows, seq)
        gate = _dwconv_tile(g_ref[...].astype(F32), prev8, next8, cw_ref[...], 1) + cb_ref[...]
        act = (_silu(gate) * v_ref[...].astype(F32)).astype(BF16)
        return _dot(act, w_ref[...])

    @pl.when(k == 0)
    def _():
        o_ref[...] = r_ref[...] + partial_product()

    @pl.when(k != 0)
    def _():
        o_ref[...] = o_ref[...] + partial_product()


def ffn_down(up, conv_w, conv_b, w_down, resid, seq, *, tm=512, tk=1408):
    total = up.shape[0]
    f, dm = w_down.shape
    nk = f // tk
    prev_spec, next_spec = _halo_specs(tm, total, tk, lambda i, k: k)
    body = functools.partial(_ffn_down_body, tile_rows=tm, seq=seq)
    return pl.pallas_call(
        body,
        grid=(total // tm, nk),
        in_specs=[
            pl.BlockSpec((tm, tk), lambda i, k: (i, k)),
            prev_spec, next_spec,
            pl.BlockSpec((tm, tk), lambda i, k: (i, nk + k)),
            pl.BlockSpec((3, tk), lambda i, k: (0, k)),
            pl.BlockSpec((1, tk), lambda i, k: (0, k)),
            pl.BlockSpec((tk, dm), lambda i, k: (k, 0)),
            pl.BlockSpec((tm, dm), lambda i, k: (i, 0)),
        ],
        out_specs=pl.BlockSpec((tm, dm), lambda i, k: (i, 0)),
        out_shape=jax.ShapeDtypeStruct((total, dm), F32),
        compiler_params=_params("parallel", "arbitrary"),
    )(up, up, up, up, conv_w, conv_b.reshape(1, f), w_down, resid)


def _rmsnorm_body(a_ref, g_ref, o_ref):
    a = a_ref[...]
    ms = jnp.mean(a * a, axis=-1, keepdims=True)
    o_ref[...] = a * lax.rsqrt(ms + NORM_EPS) * g_ref[...]


def rmsnorm_rows(a, gain, *, tm=512):
    total, dm = a.shape
    return pl.pallas_call(
        _rmsnorm_body,
        grid=(total // tm,),
        in_specs=[pl.BlockSpec((tm, dm), lambda i: (i, 0)), pl.BlockSpec((1, dm), lambda i: (0, 0))],
        out_specs=pl.BlockSpec((tm, dm), lambda i: (i, 0)),
        out_shape=jax.ShapeDtypeStruct((total, dm), F32),
        compiler_params=_params("parallel"),
    )(a, gain.reshape(1, dm))


def kernel(x, mem, mem_norm, mix_norm, xattn_norm, xattn_wq, xattn_wkv, xattn_wo, ffn_norm, ffn_w_up, ffn_conv_w, ffn_conv_b, ffn_w_down, ssd_w_in, ssd_conv_w, ssd_conv_b, ssd_dt_bias, ssd_a_log, ssd_d_skip, ssd_norm, ssd_w_out, mlstm_w_in, mlstm_gate_bias, mlstm_head_norm, mlstm_w_out, rglru_w_in, rglru_conv_w, rglru_conv_b, rglru_gate_w, rglru_gate_b, rglru_lambda, rglru_w_out, final_norm):
    batch, seq, dm = x.shape
    n_mem = mem.shape[1]
    depth = mix_norm.shape[0]
    h = x.reshape(batch * seq, dm)
    mem2 = mem.reshape(batch * n_mem, dm)
    for i in range(depth):
        kind, j = i % 3, i // 3
        if kind == 0:
            h = ssd_mixer(h, mix_norm[i], ssd_w_in[j], ssd_conv_w[j], ssd_conv_b[j], ssd_dt_bias[j],
                          ssd_a_log[j], ssd_d_skip[j], ssd_norm[j], ssd_w_out[j], batch, seq)
        elif kind == 1:
            h = mlstm_mixer(h, mix_norm[i], mlstm_w_in[j], mlstm_gate_bias[j], mlstm_head_norm[j],
                            mlstm_w_out[j], batch, seq)
        else:
            h = rglru_mixer(h, mix_norm[i], rglru_w_in[j], rglru_conv_w[j], rglru_conv_b[j],
                            rglru_gate_w[j], rglru_gate_b[j], rglru_lambda[j], rglru_w_out[j], batch, seq)
        q = fused_matmul(h, xattn_wq[i].astype(BF16), gain=xattn_norm[i], out_dtype=BF16, tm=1024, tn=1024)
        kv = fused_matmul(mem2, xattn_wkv[i].astype(BF16), gain=mem_norm, out_dtype=BF16, tm=512, tn=1024)
        att = xattn_core(q, kv.reshape(batch, n_mem, 2 * dm), seq)
        h = fused_matmul(att, xattn_wo[i].astype(BF16), resid=h, out_dtype=F32, tm=1024, tn=1024)
        up = fused_matmul(h, ffn_w_up[i].astype(BF16), gain=ffn_norm[i], out_dtype=BF16, tm=1024, tn=1408)
        h = ffn_down(up, ffn_conv_w[i], ffn_conv_b[i], ffn_w_down[i].astype(BF16), h, seq)
    return rmsnorm_rows(h, final_norm).reshape(batch, seq, dm)
```

```python
import functools
import math

import jax
import jax.numpy as jnp
from jax import lax
from jax.experimental import pallas as pl
from jax.experimental.pallas import tpu as pltpu

F32 = jnp.float32
BF16 = jnp.bfloat16

NORM_EPS = 1e-6
CHUNK = 128
HALO = 16
V7X_VMEM_LIMIT = 56 * 1024 * 1024
NEG_BIG = -1e30
LOG2_E = 1.4426950408889634

SSD_HEADS = 64
SSD_HEADDIM = 64
SSD_STATE = 128
SSD_GROUPS = 8
SSD_INNER = SSD_HEADS * SSD_HEADDIM
MLSTM_HEADS = 8
MLSTM_QK_HEAD = 128
MLSTM_V_HEAD = 256
RG_BLOCKS = 8
RG_BLOCK_DIM = 256
RG_C = 8.0
XA_HEADS = 4
XA_HEAD_DIM = 512


def _params(*sem):
    return pltpu.CompilerParams(dimension_semantics=sem, vmem_limit_bytes=V7X_VMEM_LIMIT)


def _sigmoid(x):
    return 1.0 / (1.0 + jnp.exp2(x * (-LOG2_E)))


def _softplus(x):
    return jnp.maximum(x, 0.0) + jnp.log(1.0 + jnp.exp(-jnp.abs(x)))


def _silu(x):
    return x * _sigmoid(x)


def _dot(a, b):
    return jnp.dot(a, b, preferred_element_type=F32)


def _dot_f32(a, b):
    return jnp.dot(a, b, preferred_element_type=F32, precision=lax.Precision.HIGHEST)


def _dot_nt(a, b):
    return lax.dot_general(a, b, (((1,), (1,)), ((), ())), preferred_element_type=F32)


def _dot_tn(a, b):
    return lax.dot_general(a, b, (((0,), (0,)), ((), ())), preferred_element_type=F32)


def _lane_bcast(col, width):
    return jnp.broadcast_to(col, (col.shape[0], width))


def _mm_body(*refs, has_gain, has_resid, stage_lhs):
    it = iter(refs)
    a_ref = next(it)
    g_ref = next(it) if has_gain else None
    w_ref = next(it)
    r_ref = next(it) if has_resid else None
    o_ref = next(it)
    lhs_ref = next(it) if stage_lhs else None

    if stage_lhs:
        @pl.when(pl.program_id(1) == 0)
        def _():
            a = a_ref[...].astype(F32)
            if has_gain:
                ms = jnp.mean(a * a, axis=-1, keepdims=True)
                a = a * lax.rsqrt(ms + NORM_EPS) * g_ref[...]
            lhs_ref[...] = a.astype(BF16)
        lhs = lhs_ref[...]
    else:
        lhs = a_ref[...]
    acc = _dot(lhs, w_ref[...])
    if has_resid:
        acc = acc + r_ref[...]
    o_ref[...] = acc.astype(o_ref.dtype)


def fused_matmul(a, w, *, gain=None, resid=None, out_dtype, tm, tn):
    m, k = a.shape
    n = w.shape[1]
    tm = min(tm, m)
    tn = min(tn, n)
    assert m % tm == 0 and n % tn == 0, (m, n, tm, tn)
    has_gain = gain is not None
    has_resid = resid is not None
    stage_lhs = has_gain or a.dtype != BF16
    in_specs = [pl.BlockSpec((tm, k), lambda i, j: (i, 0))]
    args = [a]
    if has_gain:
        in_specs.append(pl.BlockSpec((1, k), lambda i, j: (0, 0)))
        args.append(gain.reshape(1, k).astype(F32))
    in_specs.append(pl.BlockSpec((k, tn), lambda i, j: (0, j)))
    args.append(w)
    if has_resid:
        in_specs.append(pl.BlockSpec((tm, tn), lambda i, j: (i, j)))
        args.append(resid)
    scratch = [pltpu.VMEM((tm, k), BF16)] if stage_lhs else []
    body = functools.partial(_mm_body, has_gain=has_gain, has_resid=has_resid, stage_lhs=stage_lhs)
    return pl.pallas_call(
        body,
        grid=(m // tm, n // tn),
        in_specs=in_specs,
        out_specs=pl.BlockSpec((tm, tn), lambda i, j: (i, j)),
        out_shape=jax.ShapeDtypeStruct((m, n), out_dtype),
        scratch_shapes=scratch,
        compiler_params=_params("parallel", "arbitrary"),
    )(*args)


def _dwconv_tile(x, prev8, next8, w, left):
    rows, c = x.shape
    x3 = x.reshape(rows // 8, 8, c)
    sub = lax.broadcasted_iota(jnp.int32, x3.shape, 1)

    def shifted(d):
        if d > 0:
            r = pltpu.roll(x3, d, 1)
            nb = jnp.concatenate([pltpu.roll(prev8.reshape(1, 8, c), d, 1), r[:-1]], axis=0)
            return jnp.where(sub < d, nb, r)
        r = pltpu.roll(x3, 8 + d, 1)
        nb = jnp.concatenate([r[1:], pltpu.roll(next8.reshape(1, 8, c), 8 + d, 1)], axis=0)
        return jnp.where(sub >= 8 + d, nb, r)

    acc = None
    for t in range(w.shape[0]):
        d = left - t
        term = (x3 if d == 0 else shifted(d)) * w[t:t + 1, :]
        acc = term if acc is None else acc + term
    return acc.reshape(rows, c)


def _halo_rows(prev_ref, next_ref, tile_idx, tile_rows, seq):
    start = tile_idx * tile_rows
    keep_prev = (start % seq != 0).astype(F32)
    keep_next = ((start + tile_rows) % seq != 0).astype(F32)
    prev8 = prev_ref[HALO - 8:HALO, :].astype(F32) * keep_prev
    next8 = next_ref[0:8, :].astype(F32) * keep_next
    return prev8, next8


def _halo_specs(tile_rows, total_rows, width, col_of):
    per = tile_rows // HALO
    last = total_rows // HALO - 1
    prev = pl.BlockSpec((HALO, width), lambda *g: (jnp.maximum(g[0] * per - 1, 0), col_of(*g)))
    nxt = pl.BlockSpec((HALO, width), lambda *g: (jnp.minimum((g[0] + 1) * per, last), col_of(*g)))
    return prev, nxt


def _ssd_pre_body(x_ref, prev_ref, next_ref, w_ref, b_ref, o_ref, *, tile_rows, seq):
    prev8, next8 = _halo_rows(prev_ref, next_ref, pl.program_id(0), tile_rows, seq)
    y = _dwconv_tile(x_ref[...].astype(F32), prev8, next8, w_ref[...], 2) + b_ref[...]
    o_ref[...] = _silu(y).astype(o_ref.dtype)


def ssd_pre(proj, conv_w, conv_b, seq, *, tile_rows=512, tc=1024):
    total = proj.shape[0]
    cdim = conv_w.shape[1]
    col0 = SSD_INNER // tc
    prev_spec, next_spec = _halo_specs(tile_rows, total, tc, lambda i, c: col0 + c)
    body = functools.partial(_ssd_pre_body, tile_rows=tile_rows, seq=seq)
    return pl.pallas_call(
        body,
        grid=(total // tile_rows, cdim // tc),
        in_specs=[
            pl.BlockSpec((tile_rows, tc), lambda i, c: (i, col0 + c)),
            prev_spec, next_spec,
            pl.BlockSpec((4, tc), lambda i, c: (0, c)),
            pl.BlockSpec((1, tc), lambda i, c: (0, c)),
        ],
        out_specs=pl.BlockSpec((tile_rows, tc), lambda i, c: (i, c)),
        out_shape=jax.ShapeDtypeStruct((total, cdim), BF16),
        compiler_params=_params("parallel", "parallel"),
    )(proj, proj, proj, conv_w, conv_b.reshape(1, cdim))


def _ssd_scan_body(*refs, reverse, final):
    if final:
        (x_ref, b_ref, c_ref, dt_ref, dtb_ref, alog_ref, yb_ref, z_ref, dskip_ref,
         o_ref, state_ref) = refs
    else:
        x_ref, b_ref, c_ref, dt_ref, dtb_ref, alog_ref, o_ref, state_ref = refs
    L = CHUNK
    P2 = 2 * SSD_HEADDIM
    GW = SSD_HEADS // SSD_GROUPS * SSD_HEADDIM
    off = SSD_HEADS if reverse else 0
    edge = 0 if reverse else L - 1

    @pl.when(pl.program_id(1) == 0)
    def _():
        state_ref[...] = jnp.zeros_like(state_ref)

    row = lax.broadcasted_iota(jnp.int32, (L, L), 0)
    col = lax.broadcasted_iota(jnp.int32, (L, L), 1)
    causal = (row <= col) if reverse else (row >= col)
    lane_lo = lax.broadcasted_iota(jnp.int32, (L, P2), 1) < SSD_HEADDIM

    dt = _softplus(dt_ref[...] + dtb_ref[...])
    la = dt * (-jnp.exp(alog_ref[...]))
    cum = _dot_f32(causal.astype(F32), la) * LOG2_E
    src_t = (cum - jnp.log2(dt)).T
    total_row = cum[edge:edge + 1, :]
    w_t = (jnp.exp2(total_row - cum) * dt).T

    for g in range(SSD_GROUPS):
        bg = b_ref[:, g * SSD_STATE:(g + 1) * SSD_STATE]
        cg = c_ref[:, g * SSD_STATE:(g + 1) * SSD_STATE]
        cb = _dot_nt(cg, bg).astype(BF16)
        bg_t = bg.astype(F32).T.astype(BF16)
        y_prev = _dot(cg, state_ref[:, g * GW:(g + 1) * GW].astype(BF16))
        for j in range(GW // P2):
            lo = g * GW + j * P2
            h1 = off + (lo // SSD_HEADDIM)
            x2 = x_ref[:, lo:lo + P2]
            e_cols, m_parts, z_parts = [], [], []
            for h in (h1, h1 + 1):
                e_col = _lane_bcast(cum[:, h:h + 1], L)
                diff = jnp.where(causal, e_col - src_t[h:h + 1, :], NEG_BIG)
                m_parts.append(cb * jnp.exp2(diff).astype(BF16))
                z_parts.append(bg_t * jnp.broadcast_to(w_t[h:h + 1, :], (SSD_STATE, L)).astype(BF16))
                e_cols.append(e_col)
            lhs = jnp.concatenate([jnp.concatenate(m_parts, axis=1), jnp.concatenate(z_parts, axis=1)], axis=0)
            zero = jnp.zeros_like(x2)
            rhs = jnp.concatenate([jnp.where(lane_lo, x2, zero), jnp.where(lane_lo, zero, x2)], axis=0)
            res = _dot(lhs, rhs)
            e_pair = jnp.where(lane_lo, e_cols[0], e_cols[1])
            y = res[0:L] + jnp.exp2(e_pair) * y_prev[:, j * P2:(j + 1) * P2]
            decay = jnp.exp2(e_pair[edge:edge + 1, :])
            state_ref[:, lo:lo + P2] = state_ref[:, lo:lo + P2] * decay + res[L:2 * L]
            if final:
                y = y + yb_ref[:, lo:lo + P2] + x2.astype(F32) * dskip_ref[:, lo:lo + P2]
                y = y * _silu(z_ref[:, lo:lo + P2].astype(F32))
            o_ref[:, lo:lo + P2] = y.astype(o_ref.dtype)


def ssd_scan(xbc, dt_raw, dt_bias, a_log, batch, seq, *, reverse, y_back=None, proj=None, d_skip=None):
    final = y_back is not None
    total = xbc.shape[0]
    nc = seq // CHUNK
    bw = SSD_GROUPS * SSD_STATE

    def rows(b, c):
        return b * nc + (nc - 1 - c if reverse else c)

    in_specs = [
        pl.BlockSpec((CHUNK, SSD_INNER), lambda b, c: (rows(b, c), 0)),
        pl.BlockSpec((CHUNK, bw), lambda b, c: (rows(b, c), SSD_INNER // bw)),
        pl.BlockSpec((CHUNK, bw), lambda b, c: (rows(b, c), SSD_INNER // bw + 1)),
        pl.BlockSpec((CHUNK, 2 * SSD_HEADS), lambda b, c: (rows(b, c), 0)),
        pl.BlockSpec((1, 2 * SSD_HEADS), lambda b, c: (0, 0)),
        pl.BlockSpec((1, 2 * SSD_HEADS), lambda b, c: (0, 0)),
    ]
    args = [xbc, xbc, xbc, dt_raw, dt_bias.reshape(1, -1), a_log.reshape(1, -1)]
    if final:
        in_specs += [
            pl.BlockSpec((CHUNK, SSD_INNER), lambda b, c: (rows(b, c), 0)),
            pl.BlockSpec((CHUNK, SSD_INNER), lambda b, c: (rows(b, c), 0)),
            pl.BlockSpec((1, SSD_INNER), lambda b, c: (0, 0)),
        ]
        args += [y_back, proj, jnp.repeat(d_skip, SSD_HEADDIM).reshape(1, SSD_INNER)]
    body = functools.partial(_ssd_scan_body, reverse=reverse, final=final)
    return pl.pallas_call(
        body,
        grid=(batch, nc),
        in_specs=in_specs,
        out_specs=pl.BlockSpec((CHUNK, SSD_INNER), lambda b, c: (rows(b, c), 0)),
        out_shape=jax.ShapeDtypeStruct((total, SSD_INNER), BF16 if final else F32),
        scratch_shapes=[pltpu.VMEM((SSD_STATE, SSD_INNER), F32)],
        compiler_params=_params("parallel", "arbitrary"),
    )(*args)


def ssd_mixer(h, mix_gain, w_in, conv_w, conv_b, dt_bias, a_log, d_skip, norm_g, w_out, batch, seq):
    main = SSD_INNER + conv_w.shape[1]
    proj = fused_matmul(h, w_in[:, :main].astype(BF16), gain=mix_gain, out_dtype=BF16, tm=1024, tn=1024)
    dt_raw = fused_matmul(h, w_in[:, main:].astype(BF16), gain=mix_gain, out_dtype=F32, tm=1024, tn=128)
    xbc = ssd_pre(proj, conv_w, conv_b, seq)
    y_back = ssd_scan(xbc, dt_raw, dt_bias, a_log, batch, seq, reverse=True)
    y = ssd_scan(xbc, dt_raw, dt_bias, a_log, batch, seq, reverse=False,
                 y_back=y_back, proj=proj, d_skip=d_skip)
    return fused_matmul(y, w_out.astype(BF16), gain=norm_g, resid=h, out_dtype=F32, tm=512, tn=1024)


def _running_max(u, reverse):
    nrow = u.shape[0]
    rowi = lax.broadcasted_iota(jnp.int32, u.shape, 0)
    step = 1
    while step < nrow:
        if reverse:
            shifted = jnp.where(rowi < nrow - step, pltpu.roll(u, nrow - step, 0), NEG_BIG)
        else:
            shifted = jnp.where(rowi >= step, pltpu.roll(u, step, 0), NEG_BIG)
        u = jnp.maximum(u, shifted)
        step *= 2
    return u


def _mlstm_scan_body(*refs, reverse, final):
    if final:
        q_ref, k_ref, v_ref, gate_ref, gb_ref, hb_ref, o_ref, hn_ref, out_ref, c_state, m_state = refs
    else:
        q_ref, k_ref, v_ref, gate_ref, gb_ref, out_ref, c_state, m_state = refs
    L = CHUNK
    dk, dv = MLSTM_QK_HEAD, MLSTM_V_HEAD
    edge = 0 if reverse else L - 1
    f_off = (3 if reverse else 1) * MLSTM_HEADS
    scale = dk ** -0.5

    @pl.when(pl.program_id(1) == 0)
    def _():
        c_state[...] = jnp.zeros_like(c_state)
        m_state[...] = jnp.zeros_like(m_state)

    row = lax.broadcasted_iota(jnp.int32, (L, L), 0)
    col = lax.broadcasted_iota(jnp.int32, (L, L), 1)
    causal = (row <= col) if reverse else (row >= col)

    gates = gate_ref[...] + gb_ref[...]
    fcum = _dot_f32(causal.astype(F32), -_softplus(-gates))
    u = pltpu.roll(gates, MLSTM_HEADS, 1) - fcum
    m_prev = m_state[...]
    g_tot = fcum[edge:edge + 1, :]
    m_inter = fcum + m_prev
    m_t = jnp.maximum(m_inter, fcum + _running_max(u, reverse))
    colv = fcum - m_t + math.log(scale)
    inter = jnp.exp(m_inter - m_t)
    floor = jnp.exp(-m_t)
    a = g_tot + u
    a_max = jnp.max(a, axis=0, keepdims=True)
    w_t = (jnp.exp(a - a_max) * scale).T
    u_t = u.T
    m_new = jnp.maximum(g_tot + m_prev, a_max)
    s_old = jnp.exp(g_tot + m_prev - m_new)
    s_new = jnp.exp(a_max - m_new)
    m_state[...] = m_new
    ones = jnp.ones((L, 128), BF16)

    for h in range(MLSTM_HEADS):
        cf = f_off + h
        q = q_ref[:, h * dk:(h + 1) * dk]
        k = k_ref[:, h * dk:(h + 1) * dk]
        v_ext = jnp.concatenate([v_ref[:, h * dv:(h + 1) * dv], ones], axis=1)
        c_prev = c_state[h]

        ex = _lane_bcast(colv[:, cf:cf + 1], L) + u_t[cf:cf + 1, :]
        s = (jnp.exp(jnp.where(causal, ex, NEG_BIG)) * _dot_nt(q, k)).astype(BF16)
        inter_b = _lane_bcast(inter[:, cf:cf + 1], 128)
        tot = _dot(s, v_ext) + jnp.concatenate([inter_b] * 3, axis=1) * _dot(q, c_prev.astype(BF16))
        r = 1.0 / jnp.maximum(jnp.abs(tot[:, dv:]), _lane_bcast(floor[:, cf:cf + 1], 128))
        hout = tot[:, :dv] * jnp.concatenate([r, r], axis=1)

        wk_t = (k.astype(F32).T * w_t[cf:cf + 1, :]).astype(BF16)
        c_state[h] = s_old[:, cf:cf + 1] * c_prev + s_new[:, cf:cf + 1] * _dot(wk_t, v_ext)

        if final:
            hs = hout + hb_ref[:, h * dv:(h + 1) * dv]
            ms = jnp.mean(hs * hs, axis=-1, keepdims=True)
            hs = hs * lax.rsqrt(ms + NORM_EPS) * hn_ref[:, h * dv:(h + 1) * dv]
            hs = hs * _sigmoid(o_ref[:, h * dv:(h + 1) * dv].astype(F32))
            out_ref[:, h * dv:(h + 1) * dv] = hs.astype(out_ref.dtype)
        else:
            out_ref[:, h * dv:(h + 1) * dv] = hout


def mlstm_scan(proj, gates, gate_bias, batch, seq, *, reverse, h_back=None, head_norm=None):
    final = h_back is not None
    total = proj.shape[0]
    nc = seq // CHUNK
    qk = MLSTM_HEADS * MLSTM_QK_HEAD
    vd = MLSTM_HEADS * MLSTM_V_HEAD

    def rows(b, c):
        return b * nc + (nc - 1 - c if reverse else c)

    in_specs = [
        pl.BlockSpec((CHUNK, qk), lambda b, c: (rows(b, c), 0)),
        pl.BlockSpec((CHUNK, qk), lambda b, c: (rows(b, c), 1)),
        pl.BlockSpec((CHUNK, vd), lambda b, c: (rows(b, c), 1)),
        pl.BlockSpec((CHUNK, 128), lambda b, c: (rows(b, c), 0)),
        pl.BlockSpec((1, 128), lambda b, c: (0, 0)),
    ]
    args = [proj, proj, proj, gates, gate_bias]
    if final:
        in_specs += [
            pl.BlockSpec((CHUNK, vd), lambda b, c: (rows(b, c), 0)),
            pl.BlockSpec((CHUNK, vd), lambda b, c: (rows(b, c), 2)),
            pl.BlockSpec((1, vd), lambda b, c: (0, 0)),
        ]
        args += [h_back, proj, head_norm.reshape(1, vd)]
    body = functools.partial(_mlstm_scan_body, reverse=reverse, final=final)
    return pl.pallas_call(
        body,
        grid=(batch, nc),
        in_specs=in_specs,
        out_specs=pl.BlockSpec((CHUNK, vd), lambda b, c: (rows(b, c), 0)),
        out_shape=jax.ShapeDtypeStruct((total, vd), BF16 if final else F32),
        scratch_shapes=[
            pltpu.VMEM((MLSTM_HEADS, MLSTM_QK_HEAD, MLSTM_V_HEAD + 128), F32),
            pltpu.VMEM((1, 128), F32),
        ],
        compiler_params=_params("parallel", "arbitrary"),
    )(*args)


def mlstm_mixer(h, mix_gain, w_in, gate_bias, head_norm, w_out, batch, seq):
    main = 2 * MLSTM_HEADS * MLSTM_QK_HEAD + 2 * MLSTM_HEADS * MLSTM_V_HEAD
    ngate = 4 * MLSTM_HEADS
    proj = fused_matmul(h, w_in[:, :main].astype(BF16), gain=mix_gain, out_dtype=BF16, tm=1024, tn=1024)
    w_gate = jnp.pad(w_in[:, main:], ((0, 0), (0, 128 - ngate))).astype(BF16)
    gates = fused_matmul(h, w_gate, gain=mix_gain, out_dtype=F32, tm=1024, tn=128)
    bias = jnp.pad(gate_bias.reshape(1, ngate), ((0, 0), (0, 128 - ngate)))
    h_back = mlstm_scan(proj, gates, bias, batch, seq, reverse=True)
    y = mlstm_scan(proj, gates, bias, batch, seq, reverse=False, h_back=h_back, head_norm=head_norm)
    return fused_matmul(y, w_out.astype(BF16), resid=h, out_dtype=F32, tm=1024, tn=1024)


def _rglru_body(*refs, reverse, final, tile_rows, seq):
    if final:
        (x_ref, prev_ref, next_ref, cw_ref, cb_ref, gw_ref, gb_ref, lam_ref, hb_ref, br_ref,
         o_ref, a_buf, b_buf, carry) = refs
    else:
        (x_ref, prev_ref, next_ref, cw_ref, cb_ref, gw_ref, gb_ref, lam_ref,
         o_ref, a_buf, b_buf, carry) = refs
    width = RG_BLOCKS * RG_BLOCK_DIM
    nt = seq // tile_rows
    t = pl.program_id(1)
    tile_idx = pl.program_id(0) * nt + (nt - 1 - t if reverse else t)

    @pl.when(t == 0)
    def _():
        carry[...] = jnp.zeros_like(carry)

    prev8, next8 = _halo_rows(prev_ref, next_ref, tile_idx, tile_rows, seq)
    xc = _dwconv_tile(x_ref[...].astype(F32), prev8, next8, cw_ref[...], 2) + cb_ref[...]
    for n in range(RG_BLOCKS):
        lo = n * RG_BLOCK_DIM
        xn = xc[:, lo:lo + RG_BLOCK_DIM]
        g = _dot(xn.astype(BF16), gw_ref[n]) + gb_ref[:, 2 * lo:2 * lo + 2 * RG_BLOCK_DIM]
        r = _sigmoid(g[:, :RG_BLOCK_DIM])
        i = _sigmoid(g[:, RG_BLOCK_DIM:])
        log_a = (-RG_C) * r * _softplus(-lam_ref[:, lo:lo + RG_BLOCK_DIM])
        a_buf[:, lo:lo + RG_BLOCK_DIM] = jnp.exp(log_a)
        b_buf[:, lo:lo + RG_BLOCK_DIM] = jnp.sqrt(1.0 - jnp.exp(2.0 * log_a)) * (i * xn)

    srow = lax.broadcasted_iota(jnp.int32, (8, width), 0)
    nslab = tile_rows // 8

    def slab(s, c):
        idx = nslab - 1 - s if reverse else s
        sl = pl.ds(pl.multiple_of(idx * 8, 8), 8)
        a = a_buf[sl, :]
        b = b_buf[sl, :]
        for step in (1, 2, 4):
            if reverse:
                valid = srow < 8 - step
                shift = 8 - step
            else:
                valid = srow >= step
                shift = step
            b = jnp.where(valid, a * pltpu.roll(b, shift, 0) + b, b)
            a = jnp.where(valid, a * pltpu.roll(a, shift, 0), a)
        hs = a * c + b
        b_buf[sl, :] = hs
        return hs[0:1, :] if reverse else hs[7:8, :]

    carry[...] = lax.fori_loop(0, nslab, slab, carry[...])
    hs = b_buf[...]
    if final:
        hs = (hs + hb_ref[...]) * jax.nn.gelu(br_ref[...].astype(F32))
    o_ref[...] = hs.astype(o_ref.dtype)


def rglru_scan(proj, conv_w, conv_b, gate_w, gate_b, lam, batch, seq, *, reverse, h_back=None,
               tile_rows=256):
    final = h_back is not None
    total = proj.shape[0]
    width = RG_BLOCKS * RG_BLOCK_DIM
    nt = seq // tile_rows

    def tile(b, t):
        return b * nt + (nt - 1 - t if reverse else t)

    per = tile_rows // HALO
    last = total // HALO - 1
    in_specs = [
        pl.BlockSpec((tile_rows, width), lambda b, t: (tile(b, t), 1)),
        pl.BlockSpec((HALO, width), lambda b, t: (jnp.maximum(tile(b, t) * per - 1, 0), 1)),
        pl.BlockSpec((HALO, width), lambda b, t: (jnp.minimum((tile(b, t) + 1) * per, last), 1)),
        pl.BlockSpec((4, width), lambda b, t: (0, 0)),
        pl.BlockSpec((1, width), lambda b, t: (0, 0)),
        pl.BlockSpec((RG_BLOCKS, RG_BLOCK_DIM, 2 * RG_BLOCK_DIM), lambda b, t: (0, 0, 0)),
        pl.BlockSpec((1, 2 * width), lambda b, t: (0, 0)),
        pl.BlockSpec((1, width), lambda b, t: (0, 0)),
    ]
    args = [proj, proj, proj, conv_w, conv_b.reshape(1, width), gate_w.astype(BF16),
            gate_b.reshape(1, 2 * width), lam.reshape(1, width)]
    if final:
        in_specs += [
            pl.BlockSpec((tile_rows, width), lambda b, t: (tile(b, t), 0)),
            pl.BlockSpec((tile_rows, width), lambda b, t: (tile(b, t), 0)),
        ]
        args += [h_back, proj]
    body = functools.partial(_rglru_body, reverse=reverse, final=final, tile_rows=tile_rows, seq=seq)
    return pl.pallas_call(
        body,
        grid=(batch, nt),
        in_specs=in_specs,
        out_specs=pl.BlockSpec((tile_rows, width), lambda b, t: (tile(b, t), 0)),
        out_shape=jax.ShapeDtypeStruct((total, width), BF16 if final else F32),
        scratch_shapes=[
            pltpu.VMEM((tile_rows, width), F32),
            pltpu.VMEM((tile_rows, width), F32),
            pltpu.VMEM((1, width), F32),
        ],
        compiler_params=_params("parallel", "arbitrary"),
    )(*args)


def rglru_mixer(h, mix_gain, w_in, conv_w, conv_b, gate_w, gate_b, lam, w_out, batch, seq):
    proj = fused_matmul(h, w_in.astype(BF16), gain=mix_gain, out_dtype=BF16, tm=1024, tn=1024)
    h_back = rglru_scan(proj, conv_w, conv_b, gate_w[1], gate_b[1], lam[1], batch, seq, reverse=True)
    y = rglru_scan(proj, conv_w, conv_b, gate_w[0], gate_b[0], lam[0], batch, seq, reverse=False,
                   h_back=h_back)
    return fused_matmul(y, w_out.astype(BF16), resid=h, out_dtype=F32, tm=1024, tn=1024)


def _xattn_body(q_ref, kv_ref, o_ref):
    d = XA_HEAD_DIM
    for h in range(XA_HEADS):
        q = q_ref[:, h * d:(h + 1) * d]
        k = kv_ref[0, :, h * d:(h + 1) * d]
        v = kv_ref[0, :, (XA_HEADS + h) * d:(XA_HEADS + h + 1) * d]
        s = _dot_nt(q, k) * (d ** -0.5)
        e = jnp.exp(s - jnp.max(s, axis=-1, keepdims=True))
        p = e / jnp.sum(e, axis=-1, keepdims=True)
        o_ref[:, h * d:(h + 1) * d] = _dot(p.astype(BF16), v).astype(o_ref.dtype)


def xattn_core(q, kv, seq, *, tm=512):
    total, dm = q.shape
    n_mem = kv.shape[1]
    per_seq = seq // tm
    return pl.pallas_call(
        _xattn_body,
        grid=(total // tm,),
        in_specs=[
            pl.BlockSpec((tm, dm), lambda i: (i, 0)),
            pl.BlockSpec((1, n_mem, 2 * dm), lambda i: (i // per_seq, 0, 0)),
        ],
        out_specs=pl.BlockSpec((tm, dm), lambda i: (i, 0)),
        out_shape=jax.ShapeDtypeStruct((total, dm), BF16),
        compiler_params=_params("parallel"),
    )(q, kv)


def _ffn_down_body(*refs, tile_rows, seq, nk, has_final):
    if has_final:
        g_ref, prev_ref, next_ref, v_ref, cw_ref, cb_ref, w_ref, r_ref, fg_ref, o_ref = refs
    else:
        g_ref, prev_ref, next_ref, v_ref, cw_ref, cb_ref, w_ref, r_ref, o_ref = refs
    k = pl.program_id(1)

    def partial_product():
        prev8, next8 = _halo_rows(prev_ref, next_ref, pl.program_id(0), tile_rows, seq)
        gate = _dwconv_tile(g_ref[...].astype(F32), prev8, next8, cw_ref[...], 1) + cb_ref[...]
        g16 = gate.astype(BF16)
        act = g16 * (1.0 / (1.0 + jnp.exp2(g16 * (-LOG2_E)))) * v_ref[...]
        return _dot(act, w_ref[...])

    @pl.when(k == 0)
    def _():
        o_ref[...] = r_ref[...] + partial_product()

    @pl.when((k != 0) & (k != nk - 1) if has_final else k != 0)
    def _():
        o_ref[...] = o_ref[...] + partial_product()

    if has_final:
        @pl.when(k == nk - 1)
        def _():
            acc = o_ref[...] + partial_product()
            ms = jnp.mean(acc * acc, axis=-1, keepdims=True)
            o_ref[...] = acc * lax.rsqrt(ms + NORM_EPS) * fg_ref[...]


def ffn_down(up, conv_w, conv_b, w_down, resid, seq, *, final_gain=None, tm=512, tk=1408):
    total = up.shape[0]
    f, dm = w_down.shape
    nk = f // tk
    assert nk >= 2
    has_final = final_gain is not None
    prev_spec, next_spec = _halo_specs(tm, total, tk, lambda i, k: k)
    in_specs = [
        pl.BlockSpec((tm, tk), lambda i, k: (i, k)),
        prev_spec, next_spec,
        pl.BlockSpec((tm, tk), lambda i, k: (i, nk + k)),
        pl.BlockSpec((3, tk), lambda i, k: (0, k)),
        pl.BlockSpec((1, tk), lambda i, k: (0, k)),
        pl.BlockSpec((tk, dm), lambda i, k: (k, 0)),
        pl.BlockSpec((tm, dm), lambda i, k: (i, 0)),
    ]
    args = [up, up, up, up, conv_w, conv_b.reshape(1, f), w_down, resid]
    if has_final:
        in_specs.append(pl.BlockSpec((1, dm), lambda i, k: (0, 0)))
        args.append(final_gain.reshape(1, dm))
    body = functools.partial(_ffn_down_body, tile_rows=tm, seq=seq, nk=nk, has_final=has_final)
    return pl.pallas_call(
        body,
        grid=(total // tm, nk),
        in_specs=in_specs,
        out_specs=pl.BlockSpec((tm, dm), lambda i, k: (i, 0)),
        out_shape=jax.ShapeDtypeStruct((total, dm), F32),
        compiler_params=_params("parallel", "arbitrary"),
    )(*args)


def kernel(x, mem, mem_norm, mix_norm, xattn_norm, xattn_wq, xattn_wkv, xattn_wo, ffn_norm, ffn_w_up, ffn_conv_w, ffn_conv_b, ffn_w_down, ssd_w_in, ssd_conv_w, ssd_conv_b, ssd_dt_bias, ssd_a_log, ssd_d_skip, ssd_norm, ssd_w_out, mlstm_w_in, mlstm_gate_bias, mlstm_head_norm, mlstm_w_out, rglru_w_in, rglru_conv_w, rglru_conv_b, rglru_gate_w, rglru_gate_b, rglru_lambda, rglru_w_out, final_norm):
    batch, seq, dm = x.shape
    n_mem = mem.shape[1]
    depth = mix_norm.shape[0]
    h = x.reshape(batch * seq, dm)
    mem2 = mem.reshape(batch * n_mem, dm)
    for i in range(depth):
        kind, j = i % 3, i // 3
        if kind == 0:
            h = ssd_mixer(h, mix_norm[i], ssd_w_in[j], ssd_conv_w[j], ssd_conv_b[j], ssd_dt_bias[j],
                          ssd_a_log[j], ssd_d_skip[j], ssd_norm[j], ssd_w_out[j], batch, seq)
        elif kind == 1:
            h = mlstm_mixer(h, mix_norm[i], mlstm_w_in[j], mlstm_gate_bias[j], mlstm_head_norm[j],
                            mlstm_w_out[j], batch, seq)
        else:
            h = rglru_mixer(h, mix_norm[i], rglru_w_in[j], rglru_conv_w[j], rglru_conv_b[j],
                            rglru_gate_w[j], rglru_gate_b[j], rglru_lambda[j], rglru_w_out[j], batch, seq)
        q = fused_matmul(h, xattn_wq[i].astype(BF16), gain=xattn_norm[i], out_dtype=BF16, tm=1024, tn=1024)
        kv = fused_matmul(mem2, xattn_wkv[i].astype(BF16), gain=mem_norm, out_dtype=BF16, tm=512, tn=1024)
        att = xattn_core(q, kv.reshape(batch, n_mem, 2 * dm), seq)
        h = fused_matmul(att, xattn_wo[i].astype(BF16), resid=h, out_dtype=F32, tm=1024, tn=1024)
        up = fused_matmul(h, ffn_w_up[i].astype(BF16), gain=ffn_norm[i], out_dtype=BF16, tm=1024, tn=1408)
        h = ffn_down(up, ffn_conv_w[i], ffn_conv_b[i], ffn_w_down[i].astype(BF16), h, seq,
                     final_gain=final_norm if i == depth - 1 else None)
    return h.reshape(batch, seq, dm)
```

```python
import functools
import math

import jax
import jax.numpy as jnp
from jax import lax
from jax.experimental import pallas as pl
from jax.experimental.pallas import tpu as pltpu

F32 = jnp.float32
BF16 = jnp.bfloat16

NORM_EPS = 1e-6
CHUNK = 128
HALO = 16
V7X_VMEM_LIMIT = 56 * 1024 * 1024
NEG_BIG = -1e30
LOG2_E = 1.4426950408889634

SSD_HEADS = 64
SSD_HEADDIM = 64
SSD_STATE = 128
SSD_GROUPS = 8
SSD_INNER = SSD_HEADS * SSD_HEADDIM
MLSTM_HEADS = 8
MLSTM_QK_HEAD = 128
MLSTM_V_HEAD = 256
RG_BLOCKS = 8
RG_BLOCK_DIM = 256
RG_C = 8.0
XA_HEADS = 4
XA_HEAD_DIM = 512


def _params(*sem):
    return pltpu.CompilerParams(dimension_semantics=sem, vmem_limit_bytes=V7X_VMEM_LIMIT)


def _sigmoid(x):
    return 1.0 / (1.0 + jnp.exp2(x * (-LOG2_E)))


def _softplus(x):
    return jnp.maximum(x, 0.0) + jnp.log(1.0 + jnp.exp(-jnp.abs(x)))


def _silu(x):
    return x * _sigmoid(x)


def _dot(a, b):
    return jnp.dot(a, b, preferred_element_type=F32)


def _dot_f32(a, b):
    return jnp.dot(a, b, preferred_element_type=F32, precision=lax.Precision.HIGHEST)


def _dot_nt(a, b):
    return lax.dot_general(a, b, (((1,), (1,)), ((), ())), preferred_element_type=F32)


def _dot_tn(a, b):
    return lax.dot_general(a, b, (((0,), (0,)), ((), ())), preferred_element_type=F32)


def _lane_bcast(col, width):
    return jnp.broadcast_to(col, (col.shape[0], width))


def _cast_body(w_ref, o_ref):
    o_ref[...] = w_ref[...].astype(o_ref.dtype)


def cast_bf16(stack, layer, *, col0=0, ncols=None, tr=512, tc=1024):
    _, rows, cols = stack.shape
    ncols = cols - col0 if ncols is None else ncols
    tr = min(tr, rows)
    tc = min(tc, ncols)
    assert rows % tr == 0 and ncols % tc == 0 and col0 % tc == 0, (stack.shape, col0, ncols)
    c0 = col0 // tc
    return pl.pallas_call(
        _cast_body,
        grid=(rows // tr, ncols // tc),
        in_specs=[pl.BlockSpec((None, tr, tc), lambda i, j: (layer, i, c0 + j))],
        out_specs=pl.BlockSpec((tr, tc), lambda i, j: (i, j)),
        out_shape=jax.ShapeDtypeStruct((rows, ncols), BF16),
        compiler_params=_params("parallel", "parallel"),
    )(stack)


def _mm_body(*refs, has_gain, has_resid, stage_lhs):
    it = iter(refs)
    a_ref = next(it)
    g_ref = next(it) if has_gain else None
    w_ref = next(it)
    r_ref = next(it) if has_resid else None
    o_ref = next(it)
    lhs_ref = next(it) if stage_lhs else None

    if stage_lhs:
        @pl.when(pl.program_id(1) == 0)
        def _():
            a = a_ref[...].astype(F32)
            if has_gain:
                ms = jnp.mean(a * a, axis=-1, keepdims=True)
                a = a * lax.rsqrt(ms + NORM_EPS) * g_ref[...]
            lhs_ref[...] = a.astype(BF16)
        lhs = lhs_ref[...]
    else:
        lhs = a_ref[...]
    acc = _dot(lhs, w_ref[...])
    if has_resid:
        acc = acc + r_ref[...]
    o_ref[...] = acc.astype(o_ref.dtype)


def fused_matmul(a, w, *, gain=None, resid=None, out_dtype, tm, tn):
    m, k = a.shape
    n = w.shape[1]
    tm = min(tm, m)
    tn = min(tn, n)
    assert m % tm == 0 and n % tn == 0, (m, n, tm, tn)
    has_gain = gain is not None
    has_resid = resid is not None
    stage_lhs = has_gain or a.dtype != BF16
    in_specs = [pl.BlockSpec((tm, k), lambda i, j: (i, 0))]
    args = [a]
    if has_gain:
        in_specs.append(pl.BlockSpec((1, k), lambda i, j: (0, 0)))
        args.append(gain.reshape(1, k).astype(F32))
    in_specs.append(pl.BlockSpec((k, tn), lambda i, j: (0, j)))
    args.append(w)
    if has_resid:
        in_specs.append(pl.BlockSpec((tm, tn), lambda i, j: (i, j)))
        args.append(resid)
    scratch = [pltpu.VMEM((tm, k), BF16)] if stage_lhs else []
    body = functools.partial(_mm_body, has_gain=has_gain, has_resid=has_resid, stage_lhs=stage_lhs)
    return pl.pallas_call(
        body,
        grid=(m // tm, n // tn),
        in_specs=in_specs,
        out_specs=pl.BlockSpec((tm, tn), lambda i, j: (i, j)),
        out_shape=jax.ShapeDtypeStruct((m, n), out_dtype),
        scratch_shapes=scratch,
        compiler_params=_params("parallel", "arbitrary"),
    )(*args)


def _dwconv_tile(x, prev8, next8, w, left):
    rows, c = x.shape
    x3 = x.reshape(rows // 8, 8, c)
    sub = lax.broadcasted_iota(jnp.int32, x3.shape, 1)

    def shifted(d):
        if d > 0:
            r = pltpu.roll(x3, d, 1)
            nb = jnp.concatenate([pltpu.roll(prev8.reshape(1, 8, c), d, 1), r[:-1]], axis=0)
            return jnp.where(sub < d, nb, r)
        r = pltpu.roll(x3, 8 + d, 1)
        nb = jnp.concatenate([r[1:], pltpu.roll(next8.reshape(1, 8, c), 8 + d, 1)], axis=0)
        return jnp.where(sub >= 8 + d, nb, r)

    acc = None
    for t in range(w.shape[0]):
        d = left - t
        term = (x3 if d == 0 else shifted(d)) * w[t:t + 1, :]
        acc = term if acc is None else acc + term
    return acc.reshape(rows, c)


def _halo_rows(prev_ref, next_ref, tile_idx, tile_rows, seq):
    start = tile_idx * tile_rows
    keep_prev = (start % seq != 0).astype(F32)
    keep_next = ((start + tile_rows) % seq != 0).astype(F32)
    prev8 = prev_ref[HALO - 8:HALO, :].astype(F32) * keep_prev
    next8 = next_ref[0:8, :].astype(F32) * keep_next
    return prev8, next8


def _halo_specs(tile_rows, total_rows, width, col_of):
    per = tile_rows // HALO
    last = total_rows // HALO - 1
    prev = pl.BlockSpec((HALO, width), lambda *g: (jnp.maximum(g[0] * per - 1, 0), col_of(*g)))
    nxt = pl.BlockSpec((HALO, width), lambda *g: (jnp.minimum((g[0] + 1) * per, last), col_of(*g)))
    return prev, nxt


def _ssd_pre_body(x_ref, prev_ref, next_ref, w_ref, b_ref, o_ref, *, tile_rows, seq):
    prev8, next8 = _halo_rows(prev_ref, next_ref, pl.program_id(0), tile_rows, seq)
    y = _dwconv_tile(x_ref[...].astype(F32), prev8, next8, w_ref[...], 2) + b_ref[...]
    y16 = y.astype(o_ref.dtype)
    o_ref[...] = y16 * (1.0 / (1.0 + jnp.exp2(y16 * (-LOG2_E))))


def ssd_pre(proj, conv_w, conv_b, seq, *, tile_rows=512, tc=1024):
    total = proj.shape[0]
    cdim = conv_w.shape[1]
    col0 = SSD_INNER // tc
    prev_spec, next_spec = _halo_specs(tile_rows, total, tc, lambda i, c: col0 + c)
    body = functools.partial(_ssd_pre_body, tile_rows=tile_rows, seq=seq)
    return pl.pallas_call(
        body,
        grid=(total // tile_rows, cdim // tc),
        in_specs=[
            pl.BlockSpec((tile_rows, tc), lambda i, c: (i, col0 + c)),
            prev_spec, next_spec,
            pl.BlockSpec((4, tc), lambda i, c: (0, c)),
            pl.BlockSpec((1, tc), lambda i, c: (0, c)),
        ],
        out_specs=pl.BlockSpec((tile_rows, tc), lambda i, c: (i, c)),
        out_shape=jax.ShapeDtypeStruct((total, cdim), BF16),
        compiler_params=_params("parallel", "parallel"),
    )(proj, proj, proj, conv_w, conv_b.reshape(1, cdim))


def _ssd_scan_body(*refs, reverse, final):
    if final:
        (x_ref, b_ref, c_ref, dt_ref, dtb_ref, alog_ref, yb_ref, z_ref, dskip_ref,
         o_ref, state_ref) = refs
    else:
        x_ref, b_ref, c_ref, dt_ref, dtb_ref, alog_ref, o_ref, state_ref = refs
    L = CHUNK
    P2 = 2 * SSD_HEADDIM
    GW = SSD_HEADS // SSD_GROUPS * SSD_HEADDIM
    off = SSD_HEADS if reverse else 0
    edge = 0 if reverse else L - 1

    @pl.when(pl.program_id(1) == 0)
    def _():
        state_ref[...] = jnp.zeros_like(state_ref)

    row = lax.broadcasted_iota(jnp.int32, (L, L), 0)
    col = lax.broadcasted_iota(jnp.int32, (L, L), 1)
    causal = (row <= col) if reverse else (row >= col)
    lane_lo = lax.broadcasted_iota(jnp.int32, (L, P2), 1) < SSD_HEADDIM

    dt = _softplus(dt_ref[...] + dtb_ref[...])
    la = dt * (-jnp.exp(alog_ref[...]))
    cum = _dot_f32(causal.astype(F32), la) * LOG2_E
    src_t = (cum - jnp.log2(dt)).T
    total_row = cum[edge:edge + 1, :]
    w_t = (jnp.exp2(total_row - cum) * dt).T

    for g in range(SSD_GROUPS):
        bg = b_ref[:, g * SSD_STATE:(g + 1) * SSD_STATE]
        cg = c_ref[:, g * SSD_STATE:(g + 1) * SSD_STATE]
        cb = _dot_nt(cg, bg).astype(BF16)
        bg_t = bg.astype(F32).T.astype(BF16)
        y_prev = _dot(cg, state_ref[:, g * GW:(g + 1) * GW].astype(BF16))
        for j in range(GW // P2):
            lo = g * GW + j * P2
            h1 = off + (lo // SSD_HEADDIM)
            x2 = x_ref[:, lo:lo + P2]
            e_cols, m_parts, z_parts = [], [], []
            for h in (h1, h1 + 1):
                e_col = _lane_bcast(cum[:, h:h + 1], L)
                diff = jnp.where(causal, e_col - src_t[h:h + 1, :], NEG_BIG)
                m_parts.append(cb * jnp.exp2(diff).astype(BF16))
                z_parts.append(bg_t * jnp.broadcast_to(w_t[h:h + 1, :], (SSD_STATE, L)).astype(BF16))
                e_cols.append(e_col)
            lhs = jnp.concatenate([jnp.concatenate(m_parts, axis=1), jnp.concatenate(z_parts, axis=1)], axis=0)
            zero = jnp.zeros_like(x2)
            rhs = jnp.concatenate([jnp.where(lane_lo, x2, zero), jnp.where(lane_lo, zero, x2)], axis=0)
            res = _dot(lhs, rhs)
            e_pair = jnp.where(lane_lo, e_cols[0], e_cols[1])
            y = res[0:L] + jnp.exp2(e_pair) * y_prev[:, j * P2:(j + 1) * P2]
            decay = jnp.exp2(e_pair[edge:edge + 1, :])
            state_ref[:, lo:lo + P2] = state_ref[:, lo:lo + P2] * decay + res[L:2 * L]
            if final:
                y = y + yb_ref[:, lo:lo + P2] + x2.astype(F32) * dskip_ref[:, lo:lo + P2]
                y = y * _silu(z_ref[:, lo:lo + P2].astype(F32))
            o_ref[:, lo:lo + P2] = y.astype(o_ref.dtype)


def ssd_scan(xbc, dt_raw, dt_bias, a_log, batch, seq, *, reverse, y_back=None, proj=None, d_skip=None):
    final = y_back is not None
    total = xbc.shape[0]
    nc = seq // CHUNK
    bw = SSD_GROUPS * SSD_STATE

    def rows(b, c):
        return b * nc + (nc - 1 - c if reverse else c)

    in_specs = [
        pl.BlockSpec((CHUNK, SSD_INNER), lambda b, c: (rows(b, c), 0)),
        pl.BlockSpec((CHUNK, bw), lambda b, c: (rows(b, c), SSD_INNER // bw)),
        pl.BlockSpec((CHUNK, bw), lambda b, c: (rows(b, c), SSD_INNER // bw + 1)),
        pl.BlockSpec((CHUNK, 2 * SSD_HEADS), lambda b, c: (rows(b, c), 0)),
        pl.BlockSpec((1, 2 * SSD_HEADS), lambda b, c: (0, 0)),
        pl.BlockSpec((1, 2 * SSD_HEADS), lambda b, c: (0, 0)),
    ]
    args = [xbc, xbc, xbc, dt_raw, dt_bias.reshape(1, -1), a_log.reshape(1, -1)]
    if final:
        in_specs += [
            pl.BlockSpec((CHUNK, SSD_INNER), lambda b, c: (rows(b, c), 0)),
            pl.BlockSpec((CHUNK, SSD_INNER), lambda b, c: (rows(b, c), 0)),
            pl.BlockSpec((1, SSD_INNER), lambda b, c: (0, 0)),
        ]
        args += [y_back, proj, jnp.repeat(d_skip, SSD_HEADDIM).reshape(1, SSD_INNER)]
    body = functools.partial(_ssd_scan_body, reverse=reverse, final=final)
    return pl.pallas_call(
        body,
        grid=(batch, nc),
        in_specs=in_specs,
        out_specs=pl.BlockSpec((CHUNK, SSD_INNER), lambda b, c: (rows(b, c), 0)),
        out_shape=jax.ShapeDtypeStruct((total, SSD_INNER), BF16 if final else F32),
        scratch_shapes=[pltpu.VMEM((SSD_STATE, SSD_INNER), F32)],
        compiler_params=_params("parallel", "arbitrary"),
    )(*args)


def ssd_mixer(h, mix_gain, w_in_stack, layer, conv_w, conv_b, dt_bias, a_log, d_skip, norm_g, w_out, batch, seq):
    main = SSD_INNER + conv_w.shape[1]
    w_main = cast_bf16(w_in_stack, layer, ncols=main)
    w_dt = cast_bf16(w_in_stack, layer, col0=main)
    proj = fused_matmul(h, w_main, gain=mix_gain, out_dtype=BF16, tm=1024, tn=1024)
    dt_raw = fused_matmul(h, w_dt, gain=mix_gain, out_dtype=F32, tm=1024, tn=128)
    xbc = ssd_pre(proj, conv_w, conv_b, seq)
    y_back = ssd_scan(xbc, dt_raw, dt_bias, a_log, batch, seq, reverse=True)
    y = ssd_scan(xbc, dt_raw, dt_bias, a_log, batch, seq, reverse=False,
                 y_back=y_back, proj=proj, d_skip=d_skip)
    return fused_matmul(y, w_out, gain=norm_g, resid=h, out_dtype=F32, tm=512, tn=1024)


def _running_max(u, reverse):
    nrow = u.shape[0]
    rowi = lax.broadcasted_iota(jnp.int32, u.shape, 0)
    step = 1
    while step < nrow:
        if reverse:
            shifted = jnp.where(rowi < nrow - step, pltpu.roll(u, nrow - step, 0), NEG_BIG)
        else:
            shifted = jnp.where(rowi >= step, pltpu.roll(u, step, 0), NEG_BIG)
        u = jnp.maximum(u, shifted)
        step *= 2
    return u


def _mlstm_scan_body(*refs, reverse, final):
    if final:
        q_ref, k_ref, v_ref, gate_ref, gb_ref, hb_ref, o_ref, hn_ref, out_ref, c_state, m_state = refs
    else:
        q_ref, k_ref, v_ref, gate_ref, gb_ref, out_ref, c_state, m_state = refs
    L = CHUNK
    dk, dv = MLSTM_QK_HEAD, MLSTM_V_HEAD
    edge = 0 if reverse else L - 1
    f_off = (3 if reverse else 1) * MLSTM_HEADS
    scale = dk ** -0.5

    @pl.when(pl.program_id(1) == 0)
    def _():
        c_state[...] = jnp.zeros_like(c_state)
        m_state[...] = jnp.zeros_like(m_state)

    row = lax.broadcasted_iota(jnp.int32, (L, L), 0)
    col = lax.broadcasted_iota(jnp.int32, (L, L), 1)
    causal = (row <= col) if reverse else (row >= col)

    gates = gate_ref[...] + gb_ref[...]
    fcum = _dot_f32(causal.astype(F32), -_softplus(-gates))
    u = pltpu.roll(gates, MLSTM_HEADS, 1) - fcum
    m_prev = m_state[...]
    g_tot = fcum[edge:edge + 1, :]
    m_inter = fcum + m_prev
    m_t = jnp.maximum(m_inter, fcum + _running_max(u, reverse))
    colv = fcum - m_t + math.log(scale)
    inter = jnp.exp(m_inter - m_t)
    floor = jnp.exp(-m_t)
    a = g_tot + u
    a_max = jnp.max(a, axis=0, keepdims=True)
    w_t = (jnp.exp(a - a_max) * scale).T
    u_t = u.T
    m_new = jnp.maximum(g_tot + m_prev, a_max)
    s_old = jnp.exp(g_tot + m_prev - m_new)
    s_new = jnp.exp(a_max - m_new)
    m_state[...] = m_new
    ones = jnp.ones((L, 128), BF16)

    for h in range(MLSTM_HEADS):
        cf = f_off + h
        q = q_ref[:, h * dk:(h + 1) * dk]
        k = k_ref[:, h * dk:(h + 1) * dk]
        v_ext = jnp.concatenate([v_ref[:, h * dv:(h + 1) * dv], ones], axis=1)
        c_prev = c_state[h]

        ex = _lane_bcast(colv[:, cf:cf + 1], L) + u_t[cf:cf + 1, :]
        s = (jnp.exp(jnp.where(causal, ex, NEG_BIG)) * _dot_nt(q, k)).astype(BF16)
        inter_b = _lane_bcast(inter[:, cf:cf + 1], 128)
        tot = _dot(s, v_ext) + jnp.concatenate([inter_b] * 3, axis=1) * _dot(q, c_prev.astype(BF16))
        r = 1.0 / jnp.maximum(jnp.abs(tot[:, dv:]), _lane_bcast(floor[:, cf:cf + 1], 128))
        hout = tot[:, :dv] * jnp.concatenate([r, r], axis=1)

        wk_t = (k.astype(F32).T * w_t[cf:cf + 1, :]).astype(BF16)
        c_state[h] = s_old[:, cf:cf + 1] * c_prev + s_new[:, cf:cf + 1] * _dot(wk_t, v_ext)

        if final:
            hs = hout + hb_ref[:, h * dv:(h + 1) * dv]
            ms = jnp.mean(hs * hs, axis=-1, keepdims=True)
            hs = hs * lax.rsqrt(ms + NORM_EPS) * hn_ref[:, h * dv:(h + 1) * dv]
            hs = hs * _sigmoid(o_ref[:, h * dv:(h + 1) * dv].astype(F32))
            out_ref[:, h * dv:(h + 1) * dv] = hs.astype(out_ref.dtype)
        else:
            out_ref[:, h * dv:(h + 1) * dv] = hout


def mlstm_scan(proj, gates, gate_bias, batch, seq, *, reverse, h_back=None, head_norm=None):
    final = h_back is not None
    total = proj.shape[0]
    nc = seq // CHUNK
    qk = MLSTM_HEADS * MLSTM_QK_HEAD
    vd = MLSTM_HEADS * MLSTM_V_HEAD

    def rows(b, c):
        return b * nc + (nc - 1 - c if reverse else c)

    in_specs = [
        pl.BlockSpec((CHUNK, qk), lambda b, c: (rows(b, c), 0)),
        pl.BlockSpec((CHUNK, qk), lambda b, c: (rows(b, c), 1)),
        pl.BlockSpec((CHUNK, vd), lambda b, c: (rows(b, c), 1)),
        pl.BlockSpec((CHUNK, 128), lambda b, c: (rows(b, c), 0)),
        pl.BlockSpec((1, 128), lambda b, c: (0, 0)),
    ]
    args = [proj, proj, proj, gates, gate_bias]
    if final:
        in_specs += [
            pl.BlockSpec((CHUNK, vd), lambda b, c: (rows(b, c), 0)),
            pl.BlockSpec((CHUNK, vd), lambda b, c: (rows(b, c), 2)),
            pl.BlockSpec((1, vd), lambda b, c: (0, 0)),
        ]
        args += [h_back, proj, head_norm.reshape(1, vd)]
    body = functools.partial(_mlstm_scan_body, reverse=reverse, final=final)
    return pl.pallas_call(
        body,
        grid=(batch, nc),
        in_specs=in_specs,
        out_specs=pl.BlockSpec((CHUNK, vd), lambda b, c: (rows(b, c), 0)),
        out_shape=jax.ShapeDtypeStruct((total, vd), BF16 if final else F32),
        scratch_shapes=[
            pltpu.VMEM((MLSTM_HEADS, MLSTM_QK_HEAD, MLSTM_V_HEAD + 128), F32),
            pltpu.VMEM((1, 128), F32),
        ],
        compiler_params=_params("parallel", "arbitrary"),
    )(*args)


def mlstm_mixer(h, mix_gain, w_in_stack, layer, gate_bias, head_norm, w_out, batch, seq):
    main = 2 * MLSTM_HEADS * MLSTM_QK_HEAD + 2 * MLSTM_HEADS * MLSTM_V_HEAD
    ngate = 4 * MLSTM_HEADS
    proj = fused_matmul(h, cast_bf16(w_in_stack, layer, ncols=main), gain=mix_gain, out_dtype=BF16,
                        tm=1024, tn=1024)
    w_gate = jnp.pad(w_in_stack[layer, :, main:], ((0, 0), (0, 128 - ngate))).astype(BF16)
    gates = fused_matmul(h, w_gate, gain=mix_gain, out_dtype=F32, tm=1024, tn=128)
    bias = jnp.pad(gate_bias.reshape(1, ngate), ((0, 0), (0, 128 - ngate)))
    h_back = mlstm_scan(proj, gates, bias, batch, seq, reverse=True)
    y = mlstm_scan(proj, gates, bias, batch, seq, reverse=False, h_back=h_back, head_norm=head_norm)
    return fused_matmul(y, w_out, resid=h, out_dtype=F32, tm=1024, tn=1024)


def _rglru_body(*refs, reverse, final, tile_rows, seq):
    if final:
        (x_ref, prev_ref, next_ref, cw_ref, cb_ref, gw_ref, gb_ref, lam_ref, hb_ref, br_ref,
         o_ref, a_buf, b_buf, carry) = refs
    else:
        (x_ref, prev_ref, next_ref, cw_ref, cb_ref, gw_ref, gb_ref, lam_ref,
         o_ref, a_buf, b_buf, carry) = refs
    width = RG_BLOCKS * RG_BLOCK_DIM
    nt = seq // tile_rows
    t = pl.program_id(1)
    tile_idx = pl.program_id(0) * nt + (nt - 1 - t if reverse else t)

    @pl.when(t == 0)
    def _():
        carry[...] = jnp.zeros_like(carry)

    prev8, next8 = _halo_rows(prev_ref, next_ref, tile_idx, tile_rows, seq)
    xc = _dwconv_tile(x_ref[...].astype(F32), prev8, next8, cw_ref[...], 2) + cb_ref[...]
    for n in range(RG_BLOCKS):
        lo = n * RG_BLOCK_DIM
        xn = xc[:, lo:lo + RG_BLOCK_DIM]
        g = _dot(xn.astype(BF16), gw_ref[n]) + gb_ref[:, 2 * lo:2 * lo + 2 * RG_BLOCK_DIM]
        r = _sigmoid(g[:, :RG_BLOCK_DIM])
        i = _sigmoid(g[:, RG_BLOCK_DIM:])
        log_a = (-RG_C) * r * _softplus(-lam_ref[:, lo:lo + RG_BLOCK_DIM])
        a_buf[:, lo:lo + RG_BLOCK_DIM] = jnp.exp(log_a)
        b_buf[:, lo:lo + RG_BLOCK_DIM] = jnp.sqrt(1.0 - jnp.exp(2.0 * log_a)) * (i * xn)

    srow = lax.broadcasted_iota(jnp.int32, (8, width), 0)
    nslab = tile_rows // 8

    def slab(s, c):
        idx = nslab - 1 - s if reverse else s
        sl = pl.ds(pl.multiple_of(idx * 8, 8), 8)
        a = a_buf[sl, :]
        b = b_buf[sl, :]
        for step in (1, 2, 4):
            if reverse:
                valid = srow < 8 - step
                shift = 8 - step
            else:
                valid = srow >= step
                shift = step
            b = jnp.where(valid, a * pltpu.roll(b, shift, 0) + b, b)
            a = jnp.where(valid, a * pltpu.roll(a, shift, 0), a)
        hs = a * c + b
        b_buf[sl, :] = hs
        return hs[0:1, :] if reverse else hs[7:8, :]

    carry[...] = lax.fori_loop(0, nslab, slab, carry[...])
    hs = b_buf[...]
    if final:
        hs = (hs + hb_ref[...]) * jax.nn.gelu(br_ref[...].astype(F32))
    o_ref[...] = hs.astype(o_ref.dtype)


def rglru_scan(proj, conv_w, conv_b, gate_w, gate_b, lam, batch, seq, *, reverse, h_back=None,
               tile_rows=256):
    final = h_back is not None
    total = proj.shape[0]
    width = RG_BLOCKS * RG_BLOCK_DIM
    nt = seq // tile_rows

    def tile(b, t):
        return b * nt + (nt - 1 - t if reverse else t)

    per = tile_rows // HALO
    last = total // HALO - 1
    in_specs = [
        pl.BlockSpec((tile_rows, width), lambda b, t: (tile(b, t), 1)),
        pl.BlockSpec((HALO, width), lambda b, t: (jnp.maximum(tile(b, t) * per - 1, 0), 1)),
        pl.BlockSpec((HALO, width), lambda b, t: (jnp.minimum((tile(b, t) + 1) * per, last), 1)),
        pl.BlockSpec((4, width), lambda b, t: (0, 0)),
        pl.BlockSpec((1, width), lambda b, t: (0, 0)),
        pl.BlockSpec((RG_BLOCKS, RG_BLOCK_DIM, 2 * RG_BLOCK_DIM), lambda b, t: (0, 0, 0)),
        pl.BlockSpec((1, 2 * width), lambda b, t: (0, 0)),
        pl.BlockSpec((1, width), lambda b, t: (0, 0)),
    ]
    args = [proj, proj, proj, conv_w, conv_b.reshape(1, width), gate_w.astype(BF16),
            gate_b.reshape(1, 2 * width), lam.reshape(1, width)]
    if final:
        in_specs += [
            pl.BlockSpec((tile_rows, width), lambda b, t: (tile(b, t), 0)),
            pl.BlockSpec((tile_rows, width), lambda b, t: (tile(b, t), 0)),
        ]
        args += [h_back, proj]
    body = functools.partial(_rglru_body, reverse=reverse, final=final, tile_rows=tile_rows, seq=seq)
    return pl.pallas_call(
        body,
        grid=(batch, nt),
        in_specs=in_specs,
        out_specs=pl.BlockSpec((tile_rows, width), lambda b, t: (tile(b, t), 0)),
        out_shape=jax.ShapeDtypeStruct((total, width), BF16 if final else F32),
        scratch_shapes=[
            pltpu.VMEM((tile_rows, width), F32),
            pltpu.VMEM((tile_rows, width), F32),
            pltpu.VMEM((1, width), F32),
        ],
        compiler_params=_params("parallel", "arbitrary"),
    )(*args)


def rglru_mixer(h, mix_gain, w_in, conv_w, conv_b, gate_w, gate_b, lam, w_out, batch, seq):
    proj = fused_matmul(h, w_in, gain=mix_gain, out_dtype=BF16, tm=1024, tn=1024)
    h_back = rglru_scan(proj, conv_w, conv_b, gate_w[1], gate_b[1], lam[1], batch, seq, reverse=True)
    y = rglru_scan(proj, conv_w, conv_b, gate_w[0], gate_b[0], lam[0], batch, seq, reverse=False,
                   h_back=h_back)
    return fused_matmul(y, w_out, resid=h, out_dtype=F32, tm=1024, tn=1024)


def _xattn_body(q_ref, kv_ref, o_ref):
    d = XA_HEAD_DIM
    for h in range(XA_HEADS):
        q = q_ref[:, h * d:(h + 1) * d]
        k = kv_ref[0, :, h * d:(h + 1) * d]
        v = kv_ref[0, :, (XA_HEADS + h) * d:(XA_HEADS + h + 1) * d]
        s = _dot_nt(q, k) * (d ** -0.5)
        e = jnp.exp(s - jnp.max(s, axis=-1, keepdims=True))
        p = e / jnp.sum(e, axis=-1, keepdims=True)
        o_ref[:, h * d:(h + 1) * d] = _dot(p.astype(BF16), v).astype(o_ref.dtype)


def xattn_core(q, kv, seq, *, tm=512):
    total, dm = q.shape
    n_mem = kv.shape[1]
    per_seq = seq // tm
    return pl.pallas_call(
        _xattn_body,
        grid=(total // tm,),
        in_specs=[
            pl.BlockSpec((tm, dm), lambda i: (i, 0)),
            pl.BlockSpec((1, n_mem, 2 * dm), lambda i: (i // per_seq, 0, 0)),
        ],
        out_specs=pl.BlockSpec((tm, dm), lambda i: (i, 0)),
        out_shape=jax.ShapeDtypeStruct((total, dm), BF16),
        compiler_params=_params("parallel"),
    )(q, kv)


def _ffn_down_body(*refs, tile_rows, seq, nk, has_final):
    if has_final:
        g_ref, prev_ref, next_ref, v_ref, cw_ref, cb_ref, w_ref, r_ref, fg_ref, o_ref = refs
    else:
        g_ref, prev_ref, next_ref, v_ref, cw_ref, cb_ref, w_ref, r_ref, o_ref = refs
    k = pl.program_id(1)

    def partial_product():
        prev8, next8 = _halo_rows(prev_ref, next_ref, pl.program_id(0), tile_rows, seq)
        gate = _dwconv_tile(g_ref[...].astype(F32), prev8, next8, cw_ref[...], 1) + cb_ref[...]
        g16 = gate.astype(BF16)
        act = g16 * (1.0 / (1.0 + jnp.exp2(g16 * (-LOG2_E)))) * v_ref[...]
        return _dot(act, w_ref[...])

    @pl.when(k == 0)
    def _():
        o_ref[...] = r_ref[...] + partial_product()

    @pl.when((k != 0) & (k != nk - 1) if has_final else k != 0)
    def _():
        o_ref[...] = o_ref[...] + partial_product()

    if has_final:
        @pl.when(k == nk - 1)
        def _():
            acc = o_ref[...] + partial_product()
            ms = jnp.mean(acc * acc, axis=-1, keepdims=True)
            o_ref[...] = acc * lax.rsqrt(ms + NORM_EPS) * fg_ref[...]


def ffn_down(up, conv_w, conv_b, w_down, resid, seq, *, final_gain=None, tm=512, tk=1408):
    total = up.shape[0]
    f, dm = w_down.shape
    nk = f // tk
    assert nk >= 2
    has_final = final_gain is not None
    prev_spec, next_spec = _halo_specs(tm, total, tk, lambda i, k: k)
    in_specs = [
        pl.BlockSpec((tm, tk), lambda i, k: (i, k)),
        prev_spec, next_spec,
        pl.BlockSpec((tm, tk), lambda i, k: (i, nk + k)),
        pl.BlockSpec((3, tk), lambda i, k: (0, k)),
        pl.BlockSpec((1, tk), lambda i, k: (0, k)),
        pl.BlockSpec((tk, dm), lambda i, k: (k, 0)),
        pl.BlockSpec((tm, dm), lambda i, k: (i, 0)),
    ]
    args = [up, up, up, up, conv_w, conv_b.reshape(1, f), w_down, resid]
    if has_final:
        in_specs.append(pl.BlockSpec((1, dm), lambda i, k: (0, 0)))
        args.append(final_gain.reshape(1, dm))
    body = functools.partial(_ffn_down_body, tile_rows=tm, seq=seq, nk=nk, has_final=has_final)
    return pl.pallas_call(
        body,
        grid=(total // tm, nk),
        in_specs=in_specs,
        out_specs=pl.BlockSpec((tm, dm), lambda i, k: (i, 0)),
        out_shape=jax.ShapeDtypeStruct((total, dm), F32),
        compiler_params=_params("parallel", "arbitrary"),
    )(*args)


def kernel(x, mem, mem_norm, mix_norm, xattn_norm, xattn_wq, xattn_wkv, xattn_wo, ffn_norm, ffn_w_up, ffn_conv_w, ffn_conv_b, ffn_w_down, ssd_w_in, ssd_conv_w, ssd_conv_b, ssd_dt_bias, ssd_a_log, ssd_d_skip, ssd_norm, ssd_w_out, mlstm_w_in, mlstm_gate_bias, mlstm_head_norm, mlstm_w_out, rglru_w_in, rglru_conv_w, rglru_conv_b, rglru_gate_w, rglru_gate_b, rglru_lambda, rglru_w_out, final_norm):
    batch, seq, dm = x.shape
    n_mem = mem.shape[1]
    depth = mix_norm.shape[0]
    h = x.reshape(batch * seq, dm)
    mem2 = mem.reshape(batch * n_mem, dm)
    for i in range(depth):
        kind, j = i % 3, i // 3
        if kind == 0:
            h = ssd_mixer(h, mix_norm[i], ssd_w_in, j, ssd_conv_w[j], ssd_conv_b[j], ssd_dt_bias[j],
                          ssd_a_log[j], ssd_d_skip[j], ssd_norm[j], cast_bf16(ssd_w_out, j), batch, seq)
        elif kind == 1:
            h = mlstm_mixer(h, mix_norm[i], mlstm_w_in, j, mlstm_gate_bias[j], mlstm_head_norm[j],
                            cast_bf16(mlstm_w_out, j), batch, seq)
        else:
            h = rglru_mixer(h, mix_norm[i], cast_bf16(rglru_w_in, j), rglru_conv_w[j], rglru_conv_b[j],
                            rglru_gate_w[j], rglru_gate_b[j], rglru_lambda[j], cast_bf16(rglru_w_out, j),
                            batch, seq)
        q = fused_matmul(h, cast_bf16(xattn_wq, i), gain=xattn_norm[i], out_dtype=BF16, tm=1024, tn=1024)
        kv = fused_matmul(mem2, cast_bf16(xattn_wkv, i), gain=mem_norm, out_dtype=BF16, tm=512, tn=1024)
        att = xattn_core(q, kv.reshape(batch, n_mem, 2 * dm), seq)
        h = fused_matmul(att, cast_bf16(xattn_wo, i), resid=h, out_dtype=F32, tm=1024, tn=1024)
        up = fused_matmul(h, cast_bf16(ffn_w_up, i), gain=ffn_norm[i], out_dtype=BF16, tm=1024, tn=1408)
        h = ffn_down(up, ffn_conv_w[i], ffn_conv_b[i], cast_bf16(ffn_w_down, i), h, seq,
                     final_gain=final_norm if i == depth - 1 else None)
    return h.reshape(batch, seq, dm)
```

```python
import functools
import math

import jax
import jax.numpy as jnp
from jax import lax
from jax.experimental import pallas as pl
from jax.experimental.pallas import tpu as pltpu

F32 = jnp.float32
BF16 = jnp.bfloat16

NORM_EPS = 1e-6
CHUNK = 128
HALO = 16
V7X_VMEM_LIMIT = 56 * 1024 * 1024
NEG_BIG = -1e30
LOG2_E = 1.4426950408889634

SSD_HEADS = 64
SSD_HEADDIM = 64
SSD_STATE = 128
SSD_GROUPS = 8
SSD_INNER = SSD_HEADS * SSD_HEADDIM
MLSTM_HEADS = 8
MLSTM_QK_HEAD = 128
MLSTM_V_HEAD = 256
RG_BLOCKS = 8
RG_BLOCK_DIM = 256
RG_C = 8.0
XA_HEADS = 4
XA_HEAD_DIM = 512


def _params(*sem):
    return pltpu.CompilerParams(dimension_semantics=sem, vmem_limit_bytes=V7X_VMEM_LIMIT)


def _sigmoid(x):
    return 1.0 / (1.0 + jnp.exp2(x * (-LOG2_E)))


def _softplus(x):
    return jnp.maximum(x, 0.0) + jnp.log(1.0 + jnp.exp(-jnp.abs(x)))


def _silu(x):
    return x * _sigmoid(x)


def _dot(a, b):
    return jnp.dot(a, b, preferred_element_type=F32)


def _dot_f32(a, b):
    return jnp.dot(a, b, preferred_element_type=F32, precision=lax.Precision.HIGHEST)


def _dot_nt(a, b):
    return lax.dot_general(a, b, (((1,), (1,)), ((), ())), preferred_element_type=F32)


def _dot_tn(a, b):
    return lax.dot_general(a, b, (((0,), (0,)), ((), ())), preferred_element_type=F32)


def _lane_bcast(col, width):
    return jnp.broadcast_to(col, (col.shape[0], width))


def _cast_body(w_ref, o_ref):
    o_ref[...] = w_ref[...].astype(o_ref.dtype)


def cast_bf16(stack, layer, *, col0=0, ncols=None, tr=512, tc=1024):
    _, rows, cols = stack.shape
    ncols = cols - col0 if ncols is None else ncols
    tr = min(tr, rows)
    tc = min(tc, ncols)
    assert rows % tr == 0 and ncols % tc == 0 and col0 % tc == 0, (stack.shape, col0, ncols)
    c0 = col0 // tc
    return pl.pallas_call(
        _cast_body,
        grid=(rows // tr, ncols // tc),
        in_specs=[pl.BlockSpec((None, tr, tc), lambda i, j: (layer, i, c0 + j))],
        out_specs=pl.BlockSpec((tr, tc), lambda i, j: (i, j)),
        out_shape=jax.ShapeDtypeStruct((rows, ncols), BF16),
        compiler_params=_params("parallel", "parallel"),
    )(stack)


def _mm_body(*refs, has_gain, has_resid, stage_lhs):
    it = iter(refs)
    a_ref = next(it)
    g_ref = next(it) if has_gain else None
    w_ref = next(it)
    r_ref = next(it) if has_resid else None
    o_ref = next(it)
    lhs_ref = next(it) if stage_lhs else None

    if stage_lhs:
        @pl.when(pl.program_id(1) == 0)
        def _():
            a = a_ref[...].astype(F32)
            if has_gain:
                ms = jnp.mean(a * a, axis=-1, keepdims=True)
                a = a * lax.rsqrt(ms + NORM_EPS) * g_ref[...]
            lhs_ref[...] = a.astype(BF16)
        lhs = lhs_ref[...]
    else:
        lhs = a_ref[...]
    acc = _dot(lhs, w_ref[...])
    if has_resid:
        acc = acc + r_ref[...]
    o_ref[...] = acc.astype(o_ref.dtype)


def fused_matmul(a, w, *, gain=None, resid=None, out_dtype, tm, tn):
    m, k = a.shape
    n = w.shape[1]
    tm = min(tm, m)
    tn = min(tn, n)
    assert m % tm == 0 and n % tn == 0, (m, n, tm, tn)
    has_gain = gain is not None
    has_resid = resid is not None
    stage_lhs = has_gain or a.dtype != BF16
    in_specs = [pl.BlockSpec((tm, k), lambda i, j: (i, 0))]
    args = [a]
    if has_gain:
        in_specs.append(pl.BlockSpec((1, k), lambda i, j: (0, 0)))
        args.append(gain.reshape(1, k).astype(F32))
    in_specs.append(pl.BlockSpec((k, tn), lambda i, j: (0, j)))
    args.append(w)
    if has_resid:
        in_specs.append(pl.BlockSpec((tm, tn), lambda i, j: (i, j)))
        args.append(resid)
    scratch = [pltpu.VMEM((tm, k), BF16)] if stage_lhs else []
    body = functools.partial(_mm_body, has_gain=has_gain, has_resid=has_resid, stage_lhs=stage_lhs)
    return pl.pallas_call(
        body,
        grid=(m // tm, n // tn),
        in_specs=in_specs,
        out_specs=pl.BlockSpec((tm, tn), lambda i, j: (i, j)),
        out_shape=jax.ShapeDtypeStruct((m, n), out_dtype),
        scratch_shapes=scratch,
        compiler_params=_params("parallel", "arbitrary"),
    )(*args)


def _dwconv_tile(x, prev8, next8, w, left):
    rows, c = x.shape
    x3 = x.reshape(rows // 8, 8, c)
    sub = lax.broadcasted_iota(jnp.int32, x3.shape, 1)

    def shifted(d):
        if d > 0:
            r = pltpu.roll(x3, d, 1)
            nb = jnp.concatenate([pltpu.roll(prev8.reshape(1, 8, c), d, 1), r[:-1]], axis=0)
            return jnp.where(sub < d, nb, r)
        r = pltpu.roll(x3, 8 + d, 1)
        nb = jnp.concatenate([r[1:], pltpu.roll(next8.reshape(1, 8, c), 8 + d, 1)], axis=0)
        return jnp.where(sub >= 8 + d, nb, r)

    acc = None
    for t in range(w.shape[0]):
        d = left - t
        term = (x3 if d == 0 else shifted(d)) * w[t:t + 1, :]
        acc = term if acc is None else acc + term
    return acc.reshape(rows, c)


def _halo_rows(prev_ref, next_ref, tile_idx, tile_rows, seq):
    start = tile_idx * tile_rows
    keep_prev = (start % seq != 0).astype(F32)
    keep_next = ((start + tile_rows) % seq != 0).astype(F32)
    prev8 = prev_ref[HALO - 8:HALO, :].astype(F32) * keep_prev
    next8 = next_ref[0:8, :].astype(F32) * keep_next
    return prev8, next8


def _halo_specs(tile_rows, total_rows, width, col_of):
    per = tile_rows // HALO
    last = total_rows // HALO - 1
    prev = pl.BlockSpec((HALO, width), lambda *g: (jnp.maximum(g[0] * per - 1, 0), col_of(*g)))
    nxt = pl.BlockSpec((HALO, width), lambda *g: (jnp.minimum((g[0] + 1) * per, last), col_of(*g)))
    return prev, nxt


def _ssd_pre_body(x_ref, prev_ref, next_ref, w_ref, b_ref, o_ref, *, tile_rows, seq):
    prev8, next8 = _halo_rows(prev_ref, next_ref, pl.program_id(0), tile_rows, seq)
    y = _dwconv_tile(x_ref[...].astype(F32), prev8, next8, w_ref[...], 2) + b_ref[...]
    y16 = y.astype(o_ref.dtype)
    o_ref[...] = y16 * (1.0 / (1.0 + jnp.exp2(y16 * (-LOG2_E))))


def ssd_pre(proj, conv_w, conv_b, seq, *, tile_rows=512, tc=1024):
    total = proj.shape[0]
    cdim = conv_w.shape[1]
    col0 = SSD_INNER // tc
    prev_spec, next_spec = _halo_specs(tile_rows, total, tc, lambda i, c: col0 + c)
    body = functools.partial(_ssd_pre_body, tile_rows=tile_rows, seq=seq)
    return pl.pallas_call(
        body,
        grid=(total // tile_rows, cdim // tc),
        in_specs=[
            pl.BlockSpec((tile_rows, tc), lambda i, c: (i, col0 + c)),
            prev_spec, next_spec,
            pl.BlockSpec((4, tc), lambda i, c: (0, c)),
            pl.BlockSpec((1, tc), lambda i, c: (0, c)),
        ],
        out_specs=pl.BlockSpec((tile_rows, tc), lambda i, c: (i, c)),
        out_shape=jax.ShapeDtypeStruct((total, cdim), BF16),
        compiler_params=_params("parallel", "parallel"),
    )(proj, proj, proj, conv_w, conv_b.reshape(1, cdim))


def _ssd_scan_body(*refs, reverse, final):
    if final:
        (x_ref, b_ref, c_ref, cum_ref, src_ref, w_ref, yb_ref, z_ref, dskip_ref,
         o_ref, state_ref) = refs
    else:
        x_ref, b_ref, c_ref, cum_ref, src_ref, w_ref, o_ref, state_ref = refs
    L = CHUNK
    P2 = 2 * SSD_HEADDIM
    GW = SSD_HEADS // SSD_GROUPS * SSD_HEADDIM
    off = SSD_HEADS if reverse else 0
    edge = 0 if reverse else L - 1

    @pl.when(pl.program_id(1) == 0)
    def _():
        state_ref[...] = jnp.zeros_like(state_ref)

    row = lax.broadcasted_iota(jnp.int32, (L, L), 0)
    col = lax.broadcasted_iota(jnp.int32, (L, L), 1)
    causal = (row <= col) if reverse else (row >= col)
    lane_lo = lax.broadcasted_iota(jnp.int32, (L, P2), 1) < SSD_HEADDIM

    cum = cum_ref[...]
    src_t = src_ref[...]
    w_t = w_ref[...]

    for g in range(SSD_GROUPS):
        bg = b_ref[:, g * SSD_STATE:(g + 1) * SSD_STATE]
        cg = c_ref[:, g * SSD_STATE:(g + 1) * SSD_STATE]
        cb = _dot_nt(cg, bg).astype(BF16)
        bg_t = bg.astype(F32).T.astype(BF16)
        y_prev = _dot(cg, state_ref[:, g * GW:(g + 1) * GW].astype(BF16))
        for j in range(GW // P2):
            lo = g * GW + j * P2
            h1 = off + (lo // SSD_HEADDIM)
            x2 = x_ref[:, lo:lo + P2]
            e_cols, m_parts, z_parts = [], [], []
            for h in (h1, h1 + 1):
                e_col = _lane_bcast(cum[:, h:h + 1], L)
                diff = jnp.where(causal, e_col - src_t[h:h + 1, :], NEG_BIG)
                m_parts.append(cb * jnp.exp2(diff).astype(BF16))
                z_parts.append(bg_t * jnp.broadcast_to(w_t[h:h + 1, :], (SSD_STATE, L)).astype(BF16))
                e_cols.append(e_col)
            lhs = jnp.concatenate([jnp.concatenate(m_parts, axis=1), jnp.concatenate(z_parts, axis=1)], axis=0)
            zero = jnp.zeros_like(x2)
            rhs = jnp.concatenate([jnp.where(lane_lo, x2, zero), jnp.where(lane_lo, zero, x2)], axis=0)
            res = _dot(lhs, rhs)
            e_pair = jnp.where(lane_lo, e_cols[0], e_cols[1])
            y = res[0:L] + jnp.exp2(e_pair) * y_prev[:, j * P2:(j + 1) * P2]
            decay = jnp.exp2(e_pair[edge:edge + 1, :])
            state_ref[:, lo:lo + P2] = state_ref[:, lo:lo + P2] * decay + res[L:2 * L]
            if final:
                y = y + yb_ref[:, lo:lo + P2] + x2.astype(F32) * dskip_ref[:, lo:lo + P2]
                y = y * _silu(z_ref[:, lo:lo + P2].astype(F32))
            o_ref[:, lo:lo + P2] = y.astype(o_ref.dtype)


def _ssd_decay_body(dt_ref, dtb_ref, alog_ref, cum_ref, src_ref, w_ref):
    L = CHUNK
    row = lax.broadcasted_iota(jnp.int32, (L, L), 0)
    col = lax.broadcasted_iota(jnp.int32, (L, L), 1)
    fwd_lane = lax.broadcasted_iota(jnp.int32, (L, 2 * SSD_HEADS), 1) < SSD_HEADS
    dt = _softplus(dt_ref[...] + dtb_ref[...])
    la = dt * (-jnp.exp(alog_ref[...]))
    cum = jnp.where(fwd_lane, _dot_f32((row >= col).astype(F32), la),
                    _dot_f32((row <= col).astype(F32), la)) * LOG2_E
    total_row = jnp.where(fwd_lane[0:1, :], cum[L - 1:L, :], cum[0:1, :])
    cum_ref[...] = cum
    src_ref[...] = (cum - jnp.log2(dt)).T
    w_ref[...] = (jnp.exp2(total_row - cum) * dt).T


def ssd_decay(dt_raw, dt_bias, a_log):
    total, width = dt_raw.shape
    spec = pl.BlockSpec((CHUNK, width), lambda c: (c, 0))
    vec = pl.BlockSpec((1, width), lambda c: (0, 0))
    shape = jax.ShapeDtypeStruct((total, width), F32)
    return pl.pallas_call(
        _ssd_decay_body,
        grid=(total // CHUNK,),
        in_specs=[spec, vec, vec],
        out_specs=[spec, spec, spec],
        out_shape=[shape, shape, shape],
        compiler_params=_params("parallel"),
    )(dt_raw, dt_bias.reshape(1, -1), a_log.reshape(1, -1))


def ssd_scan(xbc, decay, batch, seq, *, reverse, y_back=None, proj=None, d_skip=None):
    final = y_back is not None
    total = xbc.shape[0]
    nc = seq // CHUNK
    bw = SSD_GROUPS * SSD_STATE

    def rows(b, c):
        return b * nc + (nc - 1 - c if reverse else c)

    in_specs = [
        pl.BlockSpec((CHUNK, SSD_INNER), lambda b, c: (rows(b, c), 0)),
        pl.BlockSpec((CHUNK, bw), lambda b, c: (rows(b, c), SSD_INNER // bw)),
        pl.BlockSpec((CHUNK, bw), lambda b, c: (rows(b, c), SSD_INNER // bw + 1)),
        pl.BlockSpec((CHUNK, 2 * SSD_HEADS), lambda b, c: (rows(b, c), 0)),
        pl.BlockSpec((CHUNK, 2 * SSD_HEADS), lambda b, c: (rows(b, c), 0)),
        pl.BlockSpec((CHUNK, 2 * SSD_HEADS), lambda b, c: (rows(b, c), 0)),
    ]
    args = [xbc, xbc, xbc, *decay]
    if final:
        in_specs += [
            pl.BlockSpec((CHUNK, SSD_INNER), lambda b, c: (rows(b, c), 0)),
            pl.BlockSpec((CHUNK, SSD_INNER), lambda b, c: (rows(b, c), 0)),
            pl.BlockSpec((1, SSD_INNER), lambda b, c: (0, 0)),
        ]
        args += [y_back, proj, jnp.repeat(d_skip, SSD_HEADDIM).reshape(1, SSD_INNER)]
    body = functools.partial(_ssd_scan_body, reverse=reverse, final=final)
    return pl.pallas_call(
        body,
        grid=(batch, nc),
        in_specs=in_specs,
        out_specs=pl.BlockSpec((CHUNK, SSD_INNER), lambda b, c: (rows(b, c), 0)),
        out_shape=jax.ShapeDtypeStruct((total, SSD_INNER), BF16 if final else F32),
        scratch_shapes=[pltpu.VMEM((SSD_STATE, SSD_INNER), F32)],
        compiler_params=_params("parallel", "arbitrary"),
    )(*args)


def ssd_mixer(h, mix_gain, w_in_stack, layer, conv_w, conv_b, dt_bias, a_log, d_skip, norm_g, w_out, batch, seq):
    main = SSD_INNER + conv_w.shape[1]
    w_main = cast_bf16(w_in_stack, layer, ncols=main)
    w_dt = cast_bf16(w_in_stack, layer, col0=main)
    proj = fused_matmul(h, w_main, gain=mix_gain, out_dtype=BF16, tm=1024, tn=1024)
    dt_raw = fused_matmul(h, w_dt, gain=mix_gain, out_dtype=F32, tm=1024, tn=128)
    xbc = ssd_pre(proj, conv_w, conv_b, seq)
    decay = ssd_decay(dt_raw, dt_bias, a_log)
    y_back = ssd_scan(xbc, decay, batch, seq, reverse=True)
    y = ssd_scan(xbc, decay, batch, seq, reverse=False, y_back=y_back, proj=proj, d_skip=d_skip)
    return fused_matmul(y, w_out, gain=norm_g, resid=h, out_dtype=F32, tm=512, tn=1024)


def _running_max(u, reverse):
    nrow = u.shape[0]
    rowi = lax.broadcasted_iota(jnp.int32, u.shape, 0)
    step = 1
    while step < nrow:
        if reverse:
            shifted = jnp.where(rowi < nrow - step, pltpu.roll(u, nrow - step, 0), NEG_BIG)
        else:
            shifted = jnp.where(rowi >= step, pltpu.roll(u, step, 0), NEG_BIG)
        u = jnp.maximum(u, shifted)
        step *= 2
    return u


def _mlstm_scan_body(*refs, reverse, final):
    if final:
        q_ref, k_ref, v_ref, gate_ref, gb_ref, hb_ref, o_ref, hn_ref, out_ref, c_state, m_state = refs
    else:
        q_ref, k_ref, v_ref, gate_ref, gb_ref, out_ref, c_state, m_state = refs
    L = CHUNK
    dk, dv = MLSTM_QK_HEAD, MLSTM_V_HEAD
    edge = 0 if reverse else L - 1
    f_off = (3 if reverse else 1) * MLSTM_HEADS
    scale = dk ** -0.5

    @pl.when(pl.program_id(1) == 0)
    def _():
        c_state[...] = jnp.zeros_like(c_state)
        m_state[...] = jnp.zeros_like(m_state)

    row = lax.broadcasted_iota(jnp.int32, (L, L), 0)
    col = lax.broadcasted_iota(jnp.int32, (L, L), 1)
    causal = (row <= col) if reverse else (row >= col)

    gates = gate_ref[...] + gb_ref[...]
    fcum = _dot_f32(causal.astype(F32), -_softplus(-gates))
    u = pltpu.roll(gates, MLSTM_HEADS, 1) - fcum
    m_prev = m_state[...]
    g_tot = fcum[edge:edge + 1, :]
    m_inter = fcum + m_prev
    m_t = jnp.maximum(m_inter, fcum + _running_max(u, reverse))
    colv = fcum - m_t + math.log(scale)
    inter = jnp.exp(m_inter - m_t)
    floor = jnp.exp(-m_t)
    a = g_tot + u
    a_max = jnp.max(a, axis=0, keepdims=True)
    w_t = (jnp.exp(a - a_max) * scale).T
    u_t = u.T
    m_new = jnp.maximum(g_tot + m_prev, a_max)
    s_old = jnp.exp(g_tot + m_prev - m_new)
    s_new = jnp.exp(a_max - m_new)
    m_state[...] = m_new
    ones = jnp.ones((L, 128), BF16)

    for h in range(MLSTM_HEADS):
        cf = f_off + h
        q = q_ref[:, h * dk:(h + 1) * dk]
        k = k_ref[:, h * dk:(h + 1) * dk]
        v_ext = jnp.concatenate([v_ref[:, h * dv:(h + 1) * dv], ones], axis=1)
        c_prev = c_state[h]

        ex = _lane_bcast(colv[:, cf:cf + 1], L) + u_t[cf:cf + 1, :]
        s = (jnp.exp(jnp.where(causal, ex, NEG_BIG)) * _dot_nt(q, k)).astype(BF16)
        inter_b = _lane_bcast(inter[:, cf:cf + 1], 128)
        tot = _dot(s, v_ext) + jnp.concatenate([inter_b] * 3, axis=1) * _dot(q, c_prev.astype(BF16))
        r = 1.0 / jnp.maximum(jnp.abs(tot[:, dv:]), _lane_bcast(floor[:, cf:cf + 1], 128))
        hout = tot[:, :dv] * jnp.concatenate([r, r], axis=1)

        wk_t = (k.astype(F32).T * w_t[cf:cf + 1, :]).astype(BF16)
        c_state[h] = s_old[:, cf:cf + 1] * c_prev + s_new[:, cf:cf + 1] * _dot(wk_t, v_ext)

        if final:
            hs = hout + hb_ref[:, h * dv:(h + 1) * dv]
            ms = jnp.mean(hs * hs, axis=-1, keepdims=True)
            hs = hs * lax.rsqrt(ms + NORM_EPS) * hn_ref[:, h * dv:(h + 1) * dv]
            hs = hs * _sigmoid(o_ref[:, h * dv:(h + 1) * dv].astype(F32))
            out_ref[:, h * dv:(h + 1) * dv] = hs.astype(out_ref.dtype)
        else:
            out_ref[:, h * dv:(h + 1) * dv] = hout


def mlstm_scan(proj, gates, gate_bias, batch, seq, *, reverse, h_back=None, head_norm=None):
    final = h_back is not None
    total = proj.shape[0]
    nc = seq // CHUNK
    qk = MLSTM_HEADS * MLSTM_QK_HEAD
    vd = MLSTM_HEADS * MLSTM_V_HEAD

    def rows(b, c):
        return b * nc + (nc - 1 - c if reverse else c)

    in_specs = [
        pl.BlockSpec((CHUNK, qk), lambda b, c: (rows(b, c), 0)),
        pl.BlockSpec((CHUNK, qk), lambda b, c: (rows(b, c), 1)),
        pl.BlockSpec((CHUNK, vd), lambda b, c: (rows(b, c), 1)),
        pl.BlockSpec((CHUNK, 128), lambda b, c: (rows(b, c), 0)),
        pl.BlockSpec((1, 128), lambda b, c: (0, 0)),
    ]
    args = [proj, proj, proj, gates, gate_bias]
    if final:
        in_specs += [
            pl.BlockSpec((CHUNK, vd), lambda b, c: (rows(b, c), 0)),
            pl.BlockSpec((CHUNK, vd), lambda b, c: (rows(b, c), 2)),
            pl.BlockSpec((1, vd), lambda b, c: (0, 0)),
        ]
        args += [h_back, proj, head_norm.reshape(1, vd)]
    body = functools.partial(_mlstm_scan_body, reverse=reverse, final=final)
    return pl.pallas_call(
        body,
        grid=(batch, nc),
        in_specs=in_specs,
        out_specs=pl.BlockSpec((CHUNK, vd), lambda b, c: (rows(b, c), 0)),
        out_shape=jax.ShapeDtypeStruct((total, vd), BF16 if final else F32),
        scratch_shapes=[
            pltpu.VMEM((MLSTM_HEADS, MLSTM_QK_HEAD, MLSTM_V_HEAD + 128), F32),
            pltpu.VMEM((1, 128), F32),
        ],
        compiler_params=_params("parallel", "arbitrary"),
    )(*args)


def mlstm_mixer(h, mix_gain, w_in_stack, layer, gate_bias, head_norm, w_out, batch, seq):
    main = 2 * MLSTM_HEADS * MLSTM_QK_HEAD + 2 * MLSTM_HEADS * MLSTM_V_HEAD
    ngate = 4 * MLSTM_HEADS
    proj = fused_matmul(h, cast_bf16(w_in_stack, layer, ncols=main), gain=mix_gain, out_dtype=BF16,
                        tm=1024, tn=1024)
    w_gate = jnp.pad(w_in_stack[layer, :, main:], ((0, 0), (0, 128 - ngate))).astype(BF16)
    gates = fused_matmul(h, w_gate, gain=mix_gain, out_dtype=F32, tm=1024, tn=128)
    bias = jnp.pad(gate_bias.reshape(1, ngate), ((0, 0), (0, 128 - ngate)))
    h_back = mlstm_scan(proj, gates, bias, batch, seq, reverse=True)
    y = mlstm_scan(proj, gates, bias, batch, seq, reverse=False, h_back=h_back, head_norm=head_norm)
    return fused_matmul(y, w_out, resid=h, out_dtype=F32, tm=1024, tn=1024)


def _rglru_body(*refs, reverse, final, tile_rows, seq):
    if final:
        (x_ref, prev_ref, next_ref, cw_ref, cb_ref, gw_ref, gb_ref, lam_ref, hb_ref, br_ref,
         o_ref, a_buf, b_buf, carry) = refs
    else:
        (x_ref, prev_ref, next_ref, cw_ref, cb_ref, gw_ref, gb_ref, lam_ref,
         o_ref, a_buf, b_buf, carry) = refs
    width = RG_BLOCKS * RG_BLOCK_DIM
    nt = seq // tile_rows
    t = pl.program_id(1)
    tile_idx = pl.program_id(0) * nt + (nt - 1 - t if reverse else t)

    @pl.when(t == 0)
    def _():
        carry[...] = jnp.zeros_like(carry)

    prev8, next8 = _halo_rows(prev_ref, next_ref, tile_idx, tile_rows, seq)
    xc = _dwconv_tile(x_ref[...].astype(F32), prev8, next8, cw_ref[...], 2) + cb_ref[...]
    for n in range(RG_BLOCKS):
        lo = n * RG_BLOCK_DIM
        xn = xc[:, lo:lo + RG_BLOCK_DIM]
        g = _dot(xn.astype(BF16), gw_ref[n]) + gb_ref[:, 2 * lo:2 * lo + 2 * RG_BLOCK_DIM]
        r = _sigmoid(g[:, :RG_BLOCK_DIM])
        i = _sigmoid(g[:, RG_BLOCK_DIM:])
        log_a = (-RG_C) * r * _softplus(-lam_ref[:, lo:lo + RG_BLOCK_DIM])
        a_buf[:, lo:lo + RG_BLOCK_DIM] = jnp.exp(log_a)
        b_buf[:, lo:lo + RG_BLOCK_DIM] = jnp.sqrt(1.0 - jnp.exp(2.0 * log_a)) * (i * xn)

    srow = lax.broadcasted_iota(jnp.int32, (8, width), 0)
    nslab = tile_rows // 8

    def slab(s, c):
        idx = nslab - 1 - s if reverse else s
        sl = pl.ds(pl.multiple_of(idx * 8, 8), 8)
        a = a_buf[sl, :]
        b = b_buf[sl, :]
        for step in (1, 2, 4):
            if reverse:
                valid = srow < 8 - step
                shift = 8 - step
            else:
                valid = srow >= step
                shift = step
            b = jnp.where(valid, a * pltpu.roll(b, shift, 0) + b, b)
            a = jnp.where(valid, a * pltpu.roll(a, shift, 0), a)
        hs = a * c + b
        b_buf[sl, :] = hs
        return hs[0:1, :] if reverse else hs[7:8, :]

    carry[...] = lax.fori_loop(0, nslab, slab, carry[...])
    hs = b_buf[...]
    if final:
        hs = (hs + hb_ref[...]) * jax.nn.gelu(br_ref[...].astype(F32))
    o_ref[...] = hs.astype(o_ref.dtype)


def rglru_scan(proj, conv_w, conv_b, gate_w, gate_b, lam, batch, seq, *, reverse, h_back=None,
               tile_rows=256):
    final = h_back is not None
    total = proj.shape[0]
    width = RG_BLOCKS * RG_BLOCK_DIM
    nt = seq // tile_rows

    def tile(b, t):
        return b * nt + (nt - 1 - t if reverse else t)

    per = tile_rows // HALO
    last = total // HALO - 1
    in_specs = [
        pl.BlockSpec((tile_rows, width), lambda b, t: (tile(b, t), 1)),
        pl.BlockSpec((HALO, width), lambda b, t: (jnp.maximum(tile(b, t) * per - 1, 0), 1)),
        pl.BlockSpec((HALO, width), lambda b, t: (jnp.minimum((tile(b, t) + 1) * per, last), 1)),
        pl.BlockSpec((4, width), lambda b, t: (0, 0)),
        pl.BlockSpec((1, width), lambda b, t: (0, 0)),
        pl.BlockSpec((RG_BLOCKS, RG_BLOCK_DIM, 2 * RG_BLOCK_DIM), lambda b, t: (0, 0, 0)),
        pl.BlockSpec((1, 2 * width), lambda b, t: (0, 0)),
        pl.BlockSpec((1, width), lambda b, t: (0, 0)),
    ]
    args = [proj, proj, proj, conv_w, conv_b.reshape(1, width), gate_w.astype(BF16),
            gate_b.reshape(1, 2 * width), lam.reshape(1, width)]
    if final:
        in_specs += [
            pl.BlockSpec((tile_rows, width), lambda b, t: (tile(b, t), 0)),
            pl.BlockSpec((tile_rows, width), lambda b, t: (tile(b, t), 0)),
        ]
        args += [h_back, proj]
    body = functools.partial(_rglru_body, reverse=reverse, final=final, tile_rows=tile_rows, seq=seq)
    return pl.pallas_call(
        body,
        grid=(batch, nt),
        in_specs=in_specs,
        out_specs=pl.BlockSpec((tile_rows, width), lambda b, t: (tile(b, t), 0)),
        out_shape=jax.ShapeDtypeStruct((total, width), BF16 if final else F32),
        scratch_shapes=[
            pltpu.VMEM((tile_rows, width), F32),
            pltpu.VMEM((tile_rows, width), F32),
            pltpu.VMEM((1, width), F32),
        ],
        compiler_params=_params("parallel", "arbitrary"),
    )(*args)


def rglru_mixer(h, mix_gain, w_in, conv_w, conv_b, gate_w, gate_b, lam, w_out, batch, seq):
    proj = fused_matmul(h, w_in, gain=mix_gain, out_dtype=BF16, tm=1024, tn=1024)
    h_back = rglru_scan(proj, conv_w, conv_b, gate_w[1], gate_b[1], lam[1], batch, seq, reverse=True)
    y = rglru_scan(proj, conv_w, conv_b, gate_w[0], gate_b[0], lam[0], batch, seq, reverse=False,
                   h_back=h_back)
    return fused_matmul(y, w_out, resid=h, out_dtype=F32, tm=1024, tn=1024)


def _xattn_body(q_ref, kv_ref, o_ref):
    d = XA_HEAD_DIM
    for h in range(XA_HEADS):
        q = q_ref[:, h * d:(h + 1) * d]
        k = kv_ref[0, :, h * d:(h + 1) * d]
        v = kv_ref[0, :, (XA_HEADS + h) * d:(XA_HEADS + h + 1) * d]
        s = _dot_nt(q, k) * (d ** -0.5)
        e = jnp.exp(s - jnp.max(s, axis=-1, keepdims=True))
        p = e / jnp.sum(e, axis=-1, keepdims=True)
        o_ref[:, h * d:(h + 1) * d] = _dot(p.astype(BF16), v).astype(o_ref.dtype)


def xattn_core(q, kv, seq, *, tm=512):
    total, dm = q.shape
    n_mem = kv.shape[1]
    per_seq = seq // tm
    return pl.pallas_call(
        _xattn_body,
        grid=(total // tm,),
        in_specs=[
            pl.BlockSpec((tm, dm), lambda i: (i, 0)),
            pl.BlockSpec((1, n_mem, 2 * dm), lambda i: (i // per_seq, 0, 0)),
        ],
        out_specs=pl.BlockSpec((tm, dm), lambda i: (i, 0)),
        out_shape=jax.ShapeDtypeStruct((total, dm), BF16),
        compiler_params=_params("parallel"),
    )(q, kv)


def _ffn_down_body(*refs, tile_rows, seq, nk, has_final):
    if has_final:
        g_ref, prev_ref, next_ref, v_ref, cw_ref, cb_ref, w_ref, r_ref, fg_ref, o_ref = refs
    else:
        g_ref, prev_ref, next_ref, v_ref, cw_ref, cb_ref, w_ref, r_ref, o_ref = refs
    k = pl.program_id(1)

    def partial_product():
        prev8, next8 = _halo_rows(prev_ref, next_ref, pl.program_id(0), tile_rows, seq)
        gate = _dwconv_tile(g_ref[...].astype(F32), prev8, next8, cw_ref[...], 1) + cb_ref[...]
        g16 = gate.astype(BF16)
        act = g16 * (1.0 / (1.0 + jnp.exp2(g16 * (-LOG2_E)))) * v_ref[...]
        return _dot(act, w_ref[...])

    @pl.when(k == 0)
    def _():
        o_ref[...] = r_ref[...] + partial_product()

    @pl.when((k != 0) & (k != nk - 1) if has_final else k != 0)
    def _():
        o_ref[...] = o_ref[...] + partial_product()

    if has_final:
        @pl.when(k == nk - 1)
        def _():
            acc = o_ref[...] + partial_product()
            ms = jnp.mean(acc * acc, axis=-1, keepdims=True)
            o_ref[...] = acc * lax.rsqrt(ms + NORM_EPS) * fg_ref[...]


def ffn_down(up, conv_w, conv_b, w_down, resid, seq, *, final_gain=None, tm=512, tk=1408):
    total = up.shape[0]
    f, dm = w_down.shape
    nk = f // tk
    assert nk >= 2
    has_final = final_gain is not None
    prev_spec, next_spec = _halo_specs(tm, total, tk, lambda i, k: k)
    in_specs = [
        pl.BlockSpec((tm, tk), lambda i, k: (i, k)),
        prev_spec, next_spec,
        pl.BlockSpec((tm, tk), lambda i, k: (i, nk + k)),
        pl.BlockSpec((3, tk), lambda i, k: (0, k)),
        pl.BlockSpec((1, tk), lambda i, k: (0, k)),
        pl.BlockSpec((tk, dm), lambda i, k: (k, 0)),
        pl.BlockSpec((tm, dm), lambda i, k: (i, 0)),
    ]
    args = [up, up, up, up, conv_w, conv_b.reshape(1, f), w_down, resid]
    if has_final:
        in_specs.append(pl.BlockSpec((1, dm), lambda i, k: (0, 0)))
        args.append(final_gain.reshape(1, dm))
    body = functools.partial(_ffn_down_body, tile_rows=tm, seq=seq, nk=nk, has_final=has_final)
    return pl.pallas_call(
        body,
        grid=(total // tm, nk),
        in_specs=in_specs,
        out_specs=pl.BlockSpec((tm, dm), lambda i, k: (i, 0)),
        out_shape=jax.ShapeDtypeStruct((total, dm), F32),
        compiler_params=_params("parallel", "arbitrary"),
    )(*args)


def kernel(x, mem, mem_norm, mix_norm, xattn_norm, xattn_wq, xattn_wkv, xattn_wo, ffn_norm, ffn_w_up, ffn_conv_w, ffn_conv_b, ffn_w_down, ssd_w_in, ssd_conv_w, ssd_conv_b, ssd_dt_bias, ssd_a_log, ssd_d_skip, ssd_norm, ssd_w_out, mlstm_w_in, mlstm_gate_bias, mlstm_head_norm, mlstm_w_out, rglru_w_in, rglru_conv_w, rglru_conv_b, rglru_gate_w, rglru_gate_b, rglru_lambda, rglru_w_out, final_norm):
    batch, seq, dm = x.shape
    n_mem = mem.shape[1]
    depth = mix_norm.shape[0]
    h = x.reshape(batch * seq, dm)
    mem2 = mem.reshape(batch * n_mem, dm)
    for i in range(depth):
        kind, j = i % 3, i // 3
        if kind == 0:
            h = ssd_mixer(h, mix_norm[i], ssd_w_in, j, ssd_conv_w[j], ssd_conv_b[j], ssd_dt_bias[j],
                          ssd_a_log[j], ssd_d_skip[j], ssd_norm[j], cast_bf16(ssd_w_out, j), batch, seq)
        elif kind == 1:
            h = mlstm_mixer(h, mix_norm[i], mlstm_w_in, j, mlstm_gate_bias[j], mlstm_head_norm[j],
                            cast_bf16(mlstm_w_out, j), batch, seq)
        else:
            h = rglru_mixer(h, mix_norm[i], cast_bf16(rglru_w_in, j), rglru_conv_w[j], rglru_conv_b[j],
                            rglru_gate_w[j], rglru_gate_b[j], rglru_lambda[j], cast_bf16(rglru_w_out, j),
                            batch, seq)
        q = fused_matmul(h, cast_bf16(xattn_wq, i), gain=xattn_norm[i], out_dtype=BF16, tm=1024, tn=1024)
        kv = fused_matmul(mem2, cast_bf16(xattn_wkv, i), gain=mem_norm, out_dtype=BF16, tm=512, tn=1024)
        att = xattn_core(q, kv.reshape(batch, n_mem, 2 * dm), seq)
        h = fused_matmul(att, cast_bf16(xattn_wo, i), resid=h, out_dtype=F32, tm=1024, tn=1024)
        up = fused_matmul(h, cast_bf16(ffn_w_up, i), gain=ffn_norm[i], out_dtype=BF16, tm=1024, tn=1408)
        h = ffn_down(up, ffn_conv_w[i], ffn_conv_b[i], cast_bf16(ffn_w_down, i), h, seq,
                     final_gain=final_norm if i == depth - 1 else None)
    return h.reshape(batch, seq, dm)
```
